```python
import math
import jax, jax.numpy as jnp
from jax import lax
import numpy as np

D_MODEL = 1024
BATCH = 4
SEQ = 4096
DEPTH = 2

ROPE_THETA = 10000.0
Q_BLOCK = 128
EPS = 1e-6
NEG_INF = -1e30
D_FF = 2816
N_MOD = 9
DIFF_HEADS = 8
DIFF_HEAD_DIM = 64
MLA_HEADS = 8
MLA_NOPE_DIM = 64
MLA_ROPE_DIM = 32
MLA_V_DIM = 128
MLA_Q_RANK = 384
MLA_KV_RANK = 256
FOX_HEADS = 16
FOX_HEAD_DIM = 64

DIFF_QK_W = DIFF_HEADS * 2 * DIFF_HEAD_DIM
DIFF_V_W = DIFF_HEADS * 2 * DIFF_HEAD_DIM
MLA_QB_W = MLA_HEADS * (MLA_NOPE_DIM + MLA_ROPE_DIM)
MLA_KVB_W = MLA_HEADS * (MLA_NOPE_DIM + MLA_V_DIM)
EVEN_IN_W = 2 * DIFF_QK_W + DIFF_V_W + MLA_Q_RANK + MLA_KV_RANK + MLA_ROPE_DIM
EVEN_OUT_W = DIFF_V_W + MLA_HEADS * MLA_V_DIM
FOX_W = FOX_HEADS * FOX_HEAD_DIM
ODD_IN_W = 4 * FOX_W + FOX_HEADS

kernel_name = 'hybrid_diff_mla_fox_macaron_adaln'


def rms_norm(x, g=None):
    xf = x.astype(jnp.float32)
    y = xf * lax.rsqrt(jnp.mean(xf * xf, axis=-1, keepdims=True) + EPS)
    if g is not None:
        y = y * g.astype(jnp.float32)
    return y.astype(x.dtype)


def modulate(x, shift, scale):
    return rms_norm(x) * (1.0 + scale) + shift


def swiglu(h, w_gate, w_up, w_down):
    return (jax.nn.silu(h @ w_gate) * (h @ w_up)) @ w_down


def rope_tables(positions, dim):
    inv = ROPE_THETA ** (-jnp.arange(0, dim, 2, dtype=jnp.float32) / dim)
    ang = positions.astype(jnp.float32)[..., None] * inv
    return jnp.cos(ang)[:, None], jnp.sin(ang)[:, None]


def apply_rope(x, cos, sin):
    xf = x.astype(jnp.float32)
    x1, x2 = jnp.split(xf, 2, axis=-1)
    return jnp.concatenate([x1 * cos - x2 * sin, x2 * cos + x1 * sin], axis=-1).astype(x.dtype)


def to_heads(t, n):
    B, S, W = t.shape
    return t.reshape(B, S, n, W // n).transpose(0, 2, 1, 3)


def merge_heads(t):
    B, H, S, d = t.shape
    return t.transpose(0, 2, 1, 3).reshape(B, S, H * d)


def split_cols(t, widths):
    idx = [int(i) for i in np.cumsum(widths)[:-1]]
    return jnp.split(t, idx, axis=-1)


def causal_attention(q, k, v, scale, log_f_cum=None):
    B, H, S, dk = q.shape
    nb = S // Q_BLOCK
    q_blocks = q.reshape(B, H, nb, Q_BLOCK, dk).transpose(2, 0, 1, 3, 4)
    c_blocks = None if log_f_cum is None else log_f_cum.reshape(B, H, nb, Q_BLOCK).transpose(2, 0, 1, 3)
    k_pos = jnp.arange(S)

    def block(args):
        i, q_blk, c_blk = args
        q_pos = i * Q_BLOCK + jnp.arange(Q_BLOCK)
        logits = jnp.einsum('bhqd,bhkd->bhqk', q_blk, k).astype(jnp.float32) * scale
        if c_blk is not None:
            logits = logits + (c_blk[..., :, None] - log_f_cum[..., None, :])
        logits = jnp.where(k_pos[None, :] <= q_pos[:, None], logits, NEG_INF)
        p = jax.nn.softmax(logits, axis=-1)
        return jnp.einsum('bhqk,bhkd->bhqd', p.astype(v.dtype), v)

    out = lax.map(block, (jnp.arange(nb), q_blocks, c_blocks))
    return out.transpose(1, 2, 0, 3, 4).reshape(B, H, S, v.shape[-1])


def ab_mixer(h, cos_a, sin_a, cos_b, sin_b, lambda_init, w_in, w_qb, w_kvb, q_lat_g, kv_lat_g,
             diff_q_g, diff_k_g, mla_q_g, mla_k_g, lam_q1, lam_k1, lam_q2, lam_k2, subln_g, w_out):
    B, S, _ = h.shape
    q_a, k_a, v_a, c_q, c_kv, k_r = split_cols(
        h @ w_in, [DIFF_QK_W, DIFF_QK_W, DIFF_V_W, MLA_Q_RANK, MLA_KV_RANK, MLA_ROPE_DIM])

    q_a = q_a.reshape(B, S, DIFF_HEADS, 2, DIFF_HEAD_DIM)
    k_a = k_a.reshape(B, S, DIFF_HEADS, 2, DIFF_HEAD_DIM)

    def qk(t, j, g):
        return apply_rope(rms_norm(t[:, :, :, j].transpose(0, 2, 1, 3), g), cos_a, sin_a)

    q1, q2 = qk(q_a, 0, diff_q_g), qk(q_a, 1, diff_q_g)
    k1, k2 = qk(k_a, 0, diff_k_g), qk(k_a, 1, diff_k_g)
    v = to_heads(v_a, DIFF_HEADS)
    lam = (jnp.exp(jnp.sum(lam_q1.astype(jnp.float32) * lam_k1.astype(jnp.float32)))
           - jnp.exp(jnp.sum(lam_q2.astype(jnp.float32) * lam_k2.astype(jnp.float32)))
           + lambda_init)
    scale_a = DIFF_HEAD_DIM ** -0.5
    o = (causal_attention(q1, k1, v, scale_a).astype(jnp.float32)
         - lam * causal_attention(q2, k2, v, scale_a).astype(jnp.float32))
    o_a = (rms_norm(o, subln_g) * (1.0 - lambda_init)).astype(h.dtype)

    q_b = to_heads(rms_norm(c_q, q_lat_g) @ w_qb, MLA_HEADS)
    kv_b = to_heads(rms_norm(c_kv, kv_lat_g) @ w_kvb, MLA_HEADS)
    k_nope, v_b = kv_b[..., :MLA_NOPE_DIM], kv_b[..., MLA_NOPE_DIM:]
    k_rope = jnp.broadcast_to(k_r[:, None], (B, MLA_HEADS, S, MLA_ROPE_DIM))
    q_b = rms_norm(q_b, mla_q_g)
    k_b = rms_norm(jnp.concatenate([k_nope, k_rope], axis=-1), mla_k_g)
    q_b = jnp.concatenate([q_b[..., :MLA_NOPE_DIM], apply_rope(q_b[..., MLA_NOPE_DIM:], cos_b, sin_b)], axis=-1)
    k_b = jnp.concatenate([k_b[..., :MLA_NOPE_DIM], apply_rope(k_b[..., MLA_NOPE_DIM:], cos_b, sin_b)], axis=-1)
    o_b = causal_attention(q_b, k_b, v_b, (MLA_NOPE_DIM + MLA_ROPE_DIM) ** -0.5)

    return jnp.concatenate([merge_heads(o_a), merge_heads(o_b)], axis=-1) @ w_out


def fox_mixer(h, w_in, b_f, q_g, k_g, w_out):
    q, k, v, og, f_logit = split_cols(h @ w_in, [FOX_W, FOX_W, FOX_W, FOX_W, FOX_HEADS])
    q = rms_norm(to_heads(q, FOX_HEADS), q_g)
    k = rms_norm(to_heads(k, FOX_HEADS), k_g)
    v = to_heads(v, FOX_HEADS)
    log_f = jax.nn.log_sigmoid((f_logit + b_f).astype(jnp.float32))
    log_f_cum = jnp.cumsum(log_f.transpose(0, 2, 1), axis=-1)
    o = causal_attention(q, k, v, FOX_HEAD_DIM ** -0.5, log_f_cum)
    return (merge_heads(o) * jax.nn.sigmoid(og)) @ w_out


def setup_inputs(seed: int = 0) -> dict:
    key = jax.random.key(seed)
    keys = jax.random.split(key, 40)
    counter = [0]

    def nk():
        counter[0] += 1
        return keys[counter[0] - 1]

    def w(shape, fan_in, mult=1.0):
        return mult * fan_in ** -0.5 * jax.random.normal(nk(), shape, jnp.float32)

    def gain(shape):
        return 1.0 + 0.1 * jax.random.normal(nk(), shape, jnp.float32)

    n_even, n_odd = (DEPTH + 1) // 2, DEPTH // 2
    D, F = D_MODEL, D_FF
    x = jax.random.normal(nk(), (BATCH, SEQ, D), jnp.float32)
    c = jax.random.normal(nk(), (BATCH, D), jnp.float32)
    positions = (jax.random.randint(nk(), (BATCH, 1), 0, 1024, jnp.int32)
                 + jnp.arange(SEQ, dtype=jnp.int32)[None, :])
    return {
        'x': x,
        'c': c,
        'positions': positions,
        'ada_w': w((DEPTH, D, N_MOD * D), D, 0.5),
        'ada_b': 0.02 * jax.random.normal(nk(), (DEPTH, N_MOD * D), jnp.float32),
        'ff1_gate': w((DEPTH, D, F), D),
        'ff1_up': w((DEPTH, D, F), D),
        'ff1_down': w((DEPTH, F, D), F),
        'ff2_gate': w((DEPTH, D, F), D),
        'ff2_up': w((DEPTH, D, F), D),
        'ff2_down': w((DEPTH, F, D), F),
        'ab_w_in': w((n_even, D, EVEN_IN_W), D),
        'mla_w_qb': w((n_even, MLA_Q_RANK, MLA_QB_W), MLA_Q_RANK),
        'mla_w_kvb': w((n_even, MLA_KV_RANK, MLA_KVB_W), MLA_KV_RANK),
        'mla_q_lat_g': gain((n_even, MLA_Q_RANK)),
        'mla_kv_lat_g': gain((n_even, MLA_KV_RANK)),
        'diff_q_g': gain((n_even, DIFF_HEAD_DIM)),
        'diff_k_g': gain((n_even, DIFF_HEAD_DIM)),
        'mla_q_g': gain((n_even, MLA_NOPE_DIM + MLA_ROPE_DIM)),
        'mla_k_g': gain((n_even, MLA_NOPE_DIM + MLA_ROPE_DIM)),
        'diff_lam_q1': 0.1 * jax.random.normal(nk(), (n_even, DIFF_HEAD_DIM), jnp.float32),
        'diff_lam_k1': 0.1 * jax.random.normal(nk(), (n_even, DIFF_HEAD_DIM), jnp.float32),
        'diff_lam_q2': 0.1 * jax.random.normal(nk(), (n_even, DIFF_HEAD_DIM), jnp.float32),
        'diff_lam_k2': 0.1 * jax.random.normal(nk(), (n_even, DIFF_HEAD_DIM), jnp.float32),
        'diff_subln_g': gain((n_even, 2 * DIFF_HEAD_DIM)),
        'ab_w_out': w((n_even, EVEN_OUT_W, D), EVEN_OUT_W),
        'fox_w_in': w((n_odd, D, ODD_IN_W), D),
        'fox_b_f': jax.random.uniform(nk(), (n_odd, FOX_HEADS), jnp.float32, 1.0, 5.0),
        'fox_q_g': gain((n_odd, FOX_HEAD_DIM)),
        'fox_k_g': gain((n_odd, FOX_HEAD_DIM)),
        'fox_w_out': w((n_odd, FOX_W, D), FOX_W),
    }


def reference(x, c, positions, ada_w, ada_b, ff1_gate, ff1_up, ff1_down, ff2_gate, ff2_up, ff2_down,
              ab_w_in, mla_w_qb, mla_w_kvb, mla_q_lat_g, mla_kv_lat_g, diff_q_g, diff_k_g, mla_q_g, mla_k_g,
              diff_lam_q1, diff_lam_k1, diff_lam_q2, diff_lam_k2, diff_subln_g, ab_w_out,
              fox_w_in, fox_b_f, fox_q_g, fox_k_g, fox_w_out):
    cos_a, sin_a = rope_tables(positions, DIFF_HEAD_DIM)
    cos_b, sin_b = rope_tables(positions, MLA_ROPE_DIM)
    cond = jax.nn.silu(c)
    for l in range(DEPTH):
        mod = cond @ ada_w[l] + ada_b[l]
        sh1, sc1, g1, sh2, sc2, g2, sh3, sc3, g3 = [m[:, None, :] for m in jnp.split(mod, N_MOD, axis=-1)]
        x = x + 0.5 * g1 * swiglu(modulate(x, sh1, sc1), ff1_gate[l], ff1_up[l], ff1_down[l])
        h = modulate(x, sh2, sc2)
        if l % 2 == 0:
            e = l // 2
            lambda_init = 0.8 - 0.6 * math.exp(-0.3 * l)
            mix = ab_mixer(h, cos_a, sin_a, cos_b, sin_b, lambda_init, ab_w_in[e], mla_w_qb[e], mla_w_kvb[e],
                           mla_q_lat_g[e], mla_kv_lat_g[e], diff_q_g[e], diff_k_g[e], mla_q_g[e], mla_k_g[e],
                           diff_lam_q1[e], diff_lam_k1[e], diff_lam_q2[e], diff_lam_k2[e], diff_subln_g[e],
                           ab_w_out[e])
        else:
            o = l // 2
            mix = fox_mixer(h, fox_w_in[o], fox_b_f[o], fox_q_g[o], fox_k_g[o], fox_w_out[o])
        x = x + g2 * mix
        x = x + 0.5 * g3 * swiglu(modulate(x, sh3, sc3), ff2_gate[l], ff2_up[l], ff2_down[l])
    return x
```

```python
import functools
import math

import numpy as np
import jax
import jax.numpy as jnp
from jax import lax
from jax.experimental import pallas as pl
from jax.experimental.pallas import tpu as pltpu

F32 = jnp.float32
BF16 = jnp.bfloat16

D_MODEL = 1024
D_FF = 2816
N_MOD = 9
ROPE_THETA = 10000.0
EPS = 1e-6
NEG_INF = -1e30
LOG2E = 1.4426950408889634

DIFF_HEADS = 8
DIFF_HEAD_DIM = 64
MLA_HEADS = 8
MLA_NOPE_DIM = 64
MLA_ROPE_DIM = 32
MLA_V_DIM = 128
MLA_Q_RANK = 384
MLA_KV_RANK = 256
FOX_HEADS = 16
FOX_HEAD_DIM = 64

DIFF_W = DIFF_HEADS * 2 * DIFF_HEAD_DIM
MLA_QK_DIM = MLA_NOPE_DIM + MLA_ROPE_DIM
FOX_W = FOX_HEADS * FOX_HEAD_DIM

LANES = 128
SLAB = 256
VMEM_LIMIT = 56 * 1024 * 1024

ROW_TILE = 512
FF_CHUNK = 256
ATT_TILE = 512

FOX_ONES_Q = (64, 65, 66)
FOX_BIAS_Q = (67, 68, 69)


def _params(sem):
    return pltpu.CompilerParams(dimension_semantics=sem, vmem_limit_bytes=VMEM_LIMIT)


def _dot(a, b):
    return jnp.dot(a, b, preferred_element_type=F32)


def _dot_nt(a, b):
    return lax.dot_general(a, b, (((1,), (1,)), ((), ())), preferred_element_type=F32)


def _sigmoid(x):
    return 1.0 / (1.0 + jnp.exp(-x))


def _modulated(x, shift, scale):
    ms = jnp.mean(x * x, axis=-1, keepdims=True)
    return x * lax.rsqrt(ms + EPS) * (1.0 + scale) + shift


def _split3(x):
    hi = x.astype(BF16)
    r = x - hi.astype(F32)
    mid = r.astype(BF16)
    lo = (r - mid.astype(F32)).astype(BF16)
    return hi, mid, lo


def _group_sums(sq, gmat):
    hi = sq.astype(BF16)
    lo = (sq - hi.astype(F32)).astype(BF16)
    return _dot(hi, gmat) + _dot(lo, gmat)


def _rope(t, cos, sin_signed, half):
    n = t.shape[-1]
    lane = lax.broadcasted_iota(jnp.int32, t.shape, 1)
    up = pltpu.roll(t, n - half, 1)
    dn = pltpu.roll(t, half, 1)
    partner = jnp.where((lane & half) == 0, up, dn)
    return t * cos + partner * sin_signed


def _adaln_kernel(c_ref, w_ref, b_ref, o_ref):
    c = c_ref[...]
    cond = (c * _sigmoid(c)).astype(BF16)
    o_ref[0] = _dot(cond, w_ref[0].astype(BF16)) + b_ref[0]


def _adaln(c_pad, ada_w, ada_b):
    depth, d, n = ada_w.shape
    tn = 1024
    return pl.pallas_call(
        _adaln_kernel,
        grid=(depth, n // tn),
        in_specs=[
            pl.BlockSpec((8, d), lambda l, j: (0, 0)),
            pl.BlockSpec((1, d, tn), lambda l, j: (l, 0, j)),
            pl.BlockSpec((1, 1, tn), lambda l, j: (l, 0, j)),
        ],
        out_specs=pl.BlockSpec((1, 8, tn), lambda l, j: (l, 0, j)),
        out_shape=jax.ShapeDtypeStruct((depth, 8, n), F32),
        compiler_params=_params(("parallel", "parallel")),
        name="adaln",
    )(c_pad, ada_w, ada_b.reshape(depth, 1, n))


def _rope_table_kernel(pos_ref, inv_ref, sgn_ref, cos_a_ref, sin_a_ref, cos_b_ref, sin_b_ref):
    ang = pos_ref[...] * inv_ref[0:1, :]
    c = jnp.cos(ang)
    s = jnp.sin(ang)
    lane = lax.broadcasted_iota(jnp.int32, ang.shape, 1)
    in_a = lane < 32
    in_b = (lane >= 64) & (lane < 80)
    rope_b = (lane >= 64) & (lane < 96)

    def spread_a(t):
        t = jnp.where(in_a, t, 0.0)
        t = t + pltpu.roll(t, 32, 1)
        return t + pltpu.roll(t, 64, 1)

    def spread_b(t):
        t = jnp.where(in_b, t, 0.0)
        return t + pltpu.roll(t, 16, 1)

    cos_a_ref[...] = spread_a(c)
    sin_a_ref[...] = spread_a(s) * sgn_ref[0:1, :]
    cos_b_ref[...] = jnp.where(rope_b, spread_b(c), 1.0)
    sin_b_ref[...] = spread_b(s) * sgn_ref[1:2, :]


def _rope_tables(pos_b, inv_rows, sgn_rows):
    t = pos_b.shape[0]
    ts = 1024
    spec = pl.BlockSpec((ts, LANES), lambda i: (i, 0))
    cspec = pl.BlockSpec((8, LANES), lambda i: (0, 0))
    out = jax.ShapeDtypeStruct((t, LANES), F32)
    return pl.pallas_call(
        _rope_table_kernel,
        grid=(t // ts,),
        in_specs=[spec, cspec, cspec],
        out_specs=[spec, spec, spec, spec],
        out_shape=[out, out, out, out],
        compiler_params=_params(("parallel",)),
        name="rope_tables",
    )(pos_b, inv_rows, sgn_rows)


def _ffn_kernel(x_ref, sh_ref, sc_ref, g_ref, wg_ref, wu_ref, wd_ref, o_ref):
    x = x_ref[...]
    h = _modulated(x, sh_ref[0], sc_ref[0]).astype(BF16)
    acc = jnp.zeros(x.shape, F32)
    for c in range(D_FF // FF_CHUNK):
        cols = slice(c * FF_CHUNK, (c + 1) * FF_CHUNK)
        g = _dot(h, wg_ref[:, cols])
        u = _dot(h, wu_ref[:, cols])
        a = (g * _sigmoid(g) * u).astype(BF16)
        acc = acc + _dot(a, wd_ref[cols, :])
    o_ref[...] = x + (0.5 * g_ref[0]) * acc


def _mod_spec(tm, seq, which):
    return pl.BlockSpec((1, 1, D_MODEL), lambda i: ((i * tm // seq) * N_MOD + which, 0, 0))


def _resident(shape):
    return pl.BlockSpec(shape, lambda i: (0,) * len(shape), pipeline_mode=pl.Buffered(1))


def _ffn(x, mod, which, wg, wu, wd, seq):
    t = x.shape[0]
    tm = ROW_TILE
    row = pl.BlockSpec((tm, D_MODEL), lambda i: (i, 0))
    return pl.pallas_call(
        _ffn_kernel,
        grid=(t // tm,),
        in_specs=[row, _mod_spec(tm, seq, which), _mod_spec(tm, seq, which + 1), _mod_spec(tm, seq, which + 2),
                  _resident(wg.shape), _resident(wu.shape), _resident(wd.shape)],
        out_specs=row,
        out_shape=jax.ShapeDtypeStruct(x.shape, F32),
        compiler_params=_params(("parallel",)),
        name="ffn",
    )(x, mod, mod, mod, wg, wu, wd)


def _ab_proj_kernel(x_ref, sh_ref, sc_ref, w_ref, wqb_ref, wkn_ref, wvb_ref, gains_ref, lat_q_ref, lat_kv_ref,
                    g64_ref, g128_ref, cos_a_ref, sin_a_ref, cos_b_ref, sin_b_ref,
                    qa_ref, ka_ref, va_ref, qb_ref, kb_ref, vb_ref):
    x = x_ref[...]
    h = _modulated(x, sh_ref[0], sc_ref[0]).astype(BF16)
    y = _dot(h, w_ref[...])
    two = lambda r: jnp.concatenate([r, r], axis=1)
    cos_a, sin_a = two(cos_a_ref[...]), two(sin_a_ref[...])
    cos_b, sin_b = two(cos_b_ref[...]), two(sin_b_ref[...])
    gq_a, gk_a = gains_ref[0:1, :], gains_ref[1:2, :]
    gq_b, gk_b = gains_ref[2:3, :], gains_ref[3:4, :]
    g64, g128 = g64_ref[...], g128_ref[...]
    scale_a = DIFF_HEAD_DIM ** -0.5 * LOG2E
    scale_b = MLA_QK_DIM ** -0.5 * LOG2E

    def head_norm(t, gmat, dim, gain):
        ss = _group_sums(t * t, gmat)
        return t * lax.rsqrt(ss * (1.0 / dim) + EPS) * gain

    for s in range(DIFF_W // SLAB):
        cols = slice(s * SLAB, (s + 1) * SLAB)
        q = _rope(head_norm(y[:, cols], g64, DIFF_HEAD_DIM, gq_a), cos_a, sin_a, 32)
        qa_ref[:, cols] = (q * scale_a).astype(BF16)
        kcols = slice(DIFF_W + s * SLAB, DIFF_W + (s + 1) * SLAB)
        k = _rope(head_norm(y[:, kcols], g64, DIFF_HEAD_DIM, gk_a), cos_a, sin_a, 32)
        ka_ref[:, cols] = k.astype(BF16)
    va_ref[...] = y[:, 2 * DIFF_W:3 * DIFF_W].astype(BF16)

    o_cq = 3 * DIFF_W
    o_ckv = o_cq + MLA_Q_RANK
    o_kr = o_ckv + MLA_KV_RANK
    c_q = y[:, o_cq:o_ckv]
    c_q = c_q * lax.rsqrt(jnp.mean(c_q * c_q, axis=-1, keepdims=True) + EPS) * lat_q_ref[...]
    q_b = _dot(c_q.astype(BF16), wqb_ref[...])
    c_kv = y[:, o_ckv:o_kr]
    c_kv = (c_kv * lax.rsqrt(jnp.mean(c_kv * c_kv, axis=-1, keepdims=True) + EPS) * lat_kv_ref[...]).astype(BF16)
    k_nope = _dot(c_kv, wkn_ref[...])
    vb_ref[...] = _dot(c_kv, wvb_ref[...]).astype(BF16)
    k_rope = two(y[:, o_kr:o_kr + LANES])
    for s in range(MLA_HEADS * LANES // SLAB):
        cols = slice(s * SLAB, (s + 1) * SLAB)
        q = _rope(head_norm(q_b[:, cols], g128, MLA_QK_DIM, gq_b), cos_b, sin_b, 16)
        qb_ref[:, cols] = (q * scale_b).astype(BF16)
        k = _rope(head_norm(k_nope[:, cols] + k_rope, g128, MLA_QK_DIM, gk_b), cos_b, sin_b, 16)
        kb_ref[:, cols] = k.astype(BF16)


def _ab_proj(x, mod, w_all, wqb, wkn, wvb, gains, lat_q, lat_kv, g64, g128, tables, seq):
    t = x.shape[0]
    tm = ROW_TILE
    row = pl.BlockSpec((tm, D_MODEL), lambda i: (i, 0))
    tab = pl.BlockSpec((tm, LANES), lambda i: (i, 0))
    out = jax.ShapeDtypeStruct((t, D_MODEL), BF16)
    return pl.pallas_call(
        _ab_proj_kernel,
        grid=(t // tm,),
        in_specs=[row, _mod_spec(tm, seq, 3), _mod_spec(tm, seq, 4),
                  _resident(w_all.shape), _resident(wqb.shape), _resident(wkn.shape), _resident(wvb.shape),
                  _resident(gains.shape), _resident(lat_q.shape), _resident(lat_kv.shape),
                  _resident(g64.shape), _resident(g128.shape), tab, tab, tab, tab],
        out_specs=[row] * 6,
        out_shape=[out] * 6,
        compiler_params=_params(("parallel",)),
        name="ab_proj",
    )(x, mod, mod, w_all, wqb, wkn, wvb, gains, lat_q, lat_kv, g64, g128, *tables)


def _fox_proj_kernel(seq, x_ref, sh_ref, sc_ref, w_ref, gains_ref, bf_ref, g64_ref, tri_ref,
                     qf_ref, kf_ref, vf_ref, og_ref, carry_ref):
    i = pl.program_id(0)
    tm = x_ref.shape[0]

    @pl.when((i * tm) % seq == 0)
    def _():
        carry_ref[...] = jnp.zeros(carry_ref.shape, F32)

    x = x_ref[...]
    h = _modulated(x, sh_ref[0], sc_ref[0]).astype(BF16)
    y = _dot(h, w_ref[...])
    vf_ref[...] = y[:, 2 * FOX_W:3 * FOX_W].astype(BF16)
    og_ref[...] = y[:, 3 * FOX_W:4 * FOX_W]

    lane = lax.broadcasted_iota(jnp.int32, (tm, LANES), 1)
    z = y[:, 4 * FOX_W:4 * FOX_W + LANES] + bf_ref[...]
    log_f = jnp.where(lane < FOX_HEADS, jnp.minimum(z, 0.0) - jnp.log1p(jnp.exp(-jnp.abs(z))), 0.0)
    tri = tri_ref[...]
    hi, mid, lo = _split3(log_f)
    cum = _dot(tri, hi) + _dot(tri, mid) + _dot(tri, lo) + carry_ref[0:1, :]
    carry_ref[0:1, :] = cum[tm - 1:tm, :]
    cum = cum * LOG2E

    gq, gk = gains_ref[0:1, :], gains_ref[1:2, :]
    g64 = g64_ref[...]
    scale = FOX_HEAD_DIM ** -0.5 * LOG2E

    def head_norm(t, gain):
        ss = _group_sums(t * t, g64)
        return t * lax.rsqrt(ss * (1.0 / FOX_HEAD_DIM) + EPS) * gain

    def bias_lanes(pieces, ones_at, pieces_at, sign):
        blk = jnp.zeros((tm, LANES), F32)
        for ln in ones_at:
            blk = jnp.where(lane == ln, 1.0, blk)
        for ln, p in zip(pieces_at, pieces):
            blk = jnp.where(lane == ln, sign * p.astype(F32), blk)
        return blk

    for s in range(FOX_W // SLAB):
        cols = slice(s * SLAB, (s + 1) * SLAB)
        qn = head_norm(y[:, cols], gq) * scale
        kn = head_norm(y[:, FOX_W + s * SLAB:FOX_W + (s + 1) * SLAB], gk)
        for j in range(SLAB // FOX_HEAD_DIM):
            hd = s * (SLAB // FOX_HEAD_DIM) + j
            blk = slice((j // 2) * LANES, (j // 2 + 1) * LANES)
            qh, kh = qn[:, blk], kn[:, blk]
            if j % 2:
                qh, kh = pltpu.roll(qh, FOX_HEAD_DIM, 1), pltpu.roll(kh, FOX_HEAD_DIM, 1)
            pieces = _split3(jnp.broadcast_to(cum[:, hd:hd + 1], (tm, LANES)))
            q_aug = bias_lanes(pieces, FOX_ONES_Q, FOX_BIAS_Q, 1.0)
            k_aug = bias_lanes(pieces, FOX_BIAS_Q, FOX_ONES_Q, -1.0)
            out = slice(hd * LANES, (hd + 1) * LANES)
            qf_ref[:, out] = jnp.where(lane < FOX_HEAD_DIM, qh, q_aug).astype(BF16)
            kf_ref[:, out] = jnp.where(lane < FOX_HEAD_DIM, kh, k_aug).astype(BF16)


def _fox_proj(x, mod, w_fox, gains, bf_row, g64, tri, seq):
    t = x.shape[0]
    tm = ROW_TILE
    row = pl.BlockSpec((tm, D_MODEL), lambda i: (i, 0))
    wide = pl.BlockSpec((tm, FOX_HEADS * LANES), lambda i: (i, 0))
    return pl.pallas_call(
        functools.partial(_fox_proj_kernel, seq),
        grid=(t // tm,),
        in_specs=[row, _mod_spec(tm, seq, 3), _mod_spec(tm, seq, 4), _resident(w_fox.shape),
                  _resident(gains.shape), _resident(bf_row.shape), _resident(g64.shape), _resident(tri.shape)],
        out_specs=[wide, wide, row, row],
        out_shape=[jax.ShapeDtypeStruct((t, FOX_HEADS * LANES), BF16)] * 2
        + [jax.ShapeDtypeStruct((t, FOX_W), BF16), jax.ShapeDtypeStruct((t, FOX_W), F32)],
        scratch_shapes=[pltpu.VMEM((8, LANES), F32)],
        compiler_params=_params(("arbitrary",)),
        name="fox_proj",
    )(x, mod, mod, w_fox, gains, bf_row, g64, tri)


def _flash(qs, k_refs, v_ref, qi, tile):
    n = len(qs)
    rows = lax.broadcasted_iota(jnp.int32, (tile, tile), 0)
    cols = lax.broadcasted_iota(jnp.int32, (tile, tile), 1)

    def step(j, carry, diagonal):
        off = pl.multiple_of(j * tile, tile)
        v = v_ref[pl.ds(off, tile), :]
        new = []
        for e in range(n):
            m, l, acc = carry[e]
            s = _dot_nt(qs[e], k_refs[e][pl.ds(off, tile), :])
            if diagonal:
                s = jnp.where(cols <= rows, s, NEG_INF)
            m_new = jnp.maximum(m, jnp.max(s, axis=-1, keepdims=True))
            alpha = jnp.exp2(m - m_new)
            p = jnp.exp2(s - m_new)
            l = alpha * l + jnp.sum(p, axis=-1, keepdims=True)
            acc = alpha * acc + _dot(p.astype(BF16), v)
            new.append((m_new, l, acc))
        return tuple(new)

    init = tuple((jnp.full((tile, 1), NEG_INF, F32), jnp.zeros((tile, 1), F32), jnp.zeros((tile, LANES), F32))
                 for _ in range(n))
    carry = lax.fori_loop(0, qi, lambda j, c: step(j, c, False), init)
    carry = step(qi, carry, True)
    return [acc / l for (_, l, acc) in carry]


def _diff_attn_kernel(lambda_init, q_ref, k_ref, v_ref, lam_ref, subln_ref, o_ref):
    tile = q_ref.shape[0]
    q = q_ref[...]
    lane = lax.broadcasted_iota(jnp.int32, q.shape, 1)
    zero = jnp.zeros_like(q)
    q1 = jnp.where(lane < DIFF_HEAD_DIM, q, zero)
    q2 = jnp.where(lane < DIFF_HEAD_DIM, zero, q)
    o1, o2 = _flash([q1, q2], [k_ref, k_ref], v_ref, pl.program_id(2), tile)
    lv = lam_ref[...]
    lam = (jnp.exp(jnp.sum(lv[0:1] * lv[1:2], axis=-1, keepdims=True))
           - jnp.exp(jnp.sum(lv[2:3] * lv[3:4], axis=-1, keepdims=True)) + lambda_init)
    o = o1 - lam * o2
    o = o * lax.rsqrt(jnp.mean(o * o, axis=-1, keepdims=True) + EPS) * subln_ref[...]
    o_ref[...] = (o * (1.0 - lambda_init)).astype(BF16)


def _mla_attn_kernel(q_ref, k_ref, v_ref, o_ref):
    (o,) = _flash([q_ref[...]], [k_ref], v_ref, pl.program_id(2), q_ref.shape[0])
    o_ref[...] = o.astype(BF16)


def _fox_attn_kernel(q1_ref, q2_ref, k1_ref, k2_ref, v_ref, og_ref, o_ref):
    tile = q1_ref.shape[0]
    o1, o2 = _flash([q1_ref[...], q2_ref[...]], [k1_ref, k2_ref], v_ref, pl.program_id(2), tile)
    lane = lax.broadcasted_iota(jnp.int32, o1.shape, 1)
    o = jnp.where(lane < FOX_HEAD_DIM, o1, o2)
    o_ref[...] = (o * _sigmoid(og_ref[...])).astype(BF16)


def _attn_specs(batch, seq, tile):
    nq = seq // tile
    q_spec = lambda f: pl.BlockSpec((tile, LANES), lambda b, h, i: (b * nq + i, f(h)))
    kv_spec = lambda f: pl.BlockSpec((seq, LANES), lambda b, h, i: (b, f(h)))
    return nq, q_spec, kv_spec


def _diff_attn(qa, ka, va, lam_rows, subln, lambda_init, batch, seq):
    tile = ATT_TILE
    nq, q_spec, kv_spec = _attn_specs(batch, seq, tile)
    same = lambda h: h
    const = lambda shape: pl.BlockSpec(shape, lambda b, h, i: (0, 0))
    return pl.pallas_call(
        functools.partial(_diff_attn_kernel, lambda_init),
        grid=(batch, DIFF_HEADS, nq),
        in_specs=[q_spec(same), kv_spec(same), kv_spec(same), const(lam_rows.shape), const(subln.shape)],
        out_specs=q_spec(same),
        out_shape=jax.ShapeDtypeStruct(qa.shape, BF16),
        compiler_params=_params(("parallel", "parallel", "arbitrary")),
        name="diff_attn",
    )(qa, ka, va, lam_rows, subln)


def _mla_attn(qb, kb, vb, batch, seq):
    tile = ATT_TILE
    nq, q_spec, kv_spec = _attn_specs(batch, seq, tile)
    same = lambda h: h
    return pl.pallas_call(
        _mla_attn_kernel,
        grid=(batch, MLA_HEADS, nq),
        in_specs=[q_spec(same), kv_spec(same), kv_spec(same)],
        out_specs=q_spec(same),
        out_shape=jax.ShapeDtypeStruct(qb.shape, BF16),
        compiler_params=_params(("parallel", "parallel", "arbitrary")),
        name="mla_attn",
    )(qb, kb, vb)


def _fox_attn(qf, kf, vf, og, batch, seq):
    tile = ATT_TILE
    nq, q_spec, kv_spec = _attn_specs(batch, seq, tile)
    even, odd, same = (lambda p: 2 * p), (lambda p: 2 * p + 1), (lambda p: p)
    return pl.pallas_call(
        _fox_attn_kernel,
        grid=(batch, FOX_HEADS // 2, nq),
        in_specs=[q_spec(even), q_spec(odd), kv_spec(even), kv_spec(odd), kv_spec(same), q_spec(same)],
        out_specs=q_spec(same),
        out_shape=jax.ShapeDtypeStruct(vf.shape, BF16),
        compiler_params=_params(("parallel", "parallel", "arbitrary")),
        name="fox_attn",
    )(qf, qf, kf, kf, vf, og)


def _out_proj_kernel(n_in, *refs):
    x_ref, g_ref = refs[0], refs[1]
    o_refs = refs[2:2 + n_in]
    w_refs = refs[2 + n_in:2 + 2 * n_in]
    out_ref = refs[2 + 2 * n_in]
    mix = _dot(o_refs[0][...], w_refs[0][...])
    for o_ref, w_ref in zip(o_refs[1:], w_refs[1:]):
        mix = mix + _dot(o_ref[...], w_ref[...])
    out_ref[...] = x_ref[...] + g_ref[0] * mix


def _out_proj(x, mod, os, ws, seq):
    t = x.shape[0]
    tm = ROW_TILE
    row = pl.BlockSpec((tm, D_MODEL), lambda i: (i, 0))
    return pl.pallas_call(
        functools.partial(_out_proj_kernel, len(os)),
        grid=(t // tm,),
        in_specs=[row, _mod_spec(tm, seq, 5)] + [pl.BlockSpec((tm, o.shape[1]), lambda i: (i, 0)) for o in os]
        + [_resident(w.shape) for w in ws],
        out_specs=row,
        out_shape=jax.ShapeDtypeStruct(x.shape, F32),
        compiler_params=_params(("parallel",)),
        name="out_proj",
    )(x, mod, *os, *ws)


def _lane_rows(rows):
    width = max(r.shape[0] for r in rows)
    out = jnp.zeros((8, width), F32)
    for i, r in enumerate(rows):
        out = out.at[i, :r.shape[0]].set(r.astype(F32))
    return out


def _block_diag_ones(n, group):
    idx = np.arange(n) // group
    return jnp.asarray((idx[:, None] == idx[None, :]).astype(np.float32), BF16)


def _pad_heads(w, heads, dim):
    r = w.shape[0]
    w = w.reshape(r, heads, dim)
    return jnp.pad(w, ((0, 0), (0, 0), (0, LANES - dim))).reshape(r, heads * LANES)


def kernel(x, c, positions, ada_w, ada_b, ff1_gate, ff1_up, ff1_down, ff2_gate, ff2_up, ff2_down, ab_w_in, mla_w_qb, mla_w_kvb, mla_q_lat_g, mla_kv_lat_g, diff_q_g, diff_k_g, mla_q_g, mla_k_g, diff_lam_q1, diff_lam_k1, diff_lam_q2, diff_lam_k2, diff_subln_g, ab_w_out, fox_w_in, fox_b_f, fox_q_g, fox_k_g, fox_w_out):
    batch, seq, d = x.shape
    depth = ada_w.shape[0]
    t = batch * seq
    assert d == D_MODEL and seq % ATT_TILE == 0 and seq % ROW_TILE == 0 and batch <= 8

    c_pad = jnp.zeros((8, d), F32).at[:batch].set(c)
    mod_all = _adaln(c_pad, ada_w, ada_b)

    inv_a = ROPE_THETA ** (-jnp.arange(0, DIFF_HEAD_DIM, 2, dtype=F32) / DIFF_HEAD_DIM)
    inv_b = ROPE_THETA ** (-jnp.arange(0, MLA_ROPE_DIM, 2, dtype=F32) / MLA_ROPE_DIM)
    inv_rows = jnp.zeros((8, LANES), F32).at[0, :32].set(inv_a).at[0, 64:80].set(inv_b)
    lane = np.arange(LANES)
    sgn_a = np.where(lane % 64 < 32, -1.0, 1.0)
    sgn_b = np.where((lane >= 64) & (lane < 80), -1.0, np.where((lane >= 80) & (lane < 96), 1.0, 0.0))
    sgn_rows = jnp.asarray(np.stack([sgn_a, sgn_b] + [np.zeros(LANES)] * 6), F32)
    pos_b = jnp.broadcast_to(positions.astype(F32).reshape(t, 1), (t, LANES))
    tables = _rope_tables(pos_b, inv_rows, sgn_rows)

    g64 = _block_diag_ones(SLAB, 64)
    g128 = _block_diag_ones(SLAB, LANES)
    bf = lambda w: w.astype(BF16)

    xt = x.reshape(t, d)
    for l in range(depth):
        mod = mod_all[l, :batch].reshape(batch * N_MOD, 1, d)
        xt = _ffn(xt, mod, 0, bf(ff1_gate[l]), bf(ff1_up[l]), bf(ff1_down[l]), seq)
        if l % 2 == 0:
            e = l // 2
            lambda_init = 0.8 - 0.6 * math.exp(-0.3 * l)
            w_in = ab_w_in[e]
            o_cq = 3 * DIFF_W
            o_kr = o_cq + MLA_Q_RANK + MLA_KV_RANK
            kr_pad = jnp.zeros((d, LANES), F32).at[:, MLA_NOPE_DIM:MLA_QK_DIM].set(w_in[:, o_kr:])
            w_all = bf(jnp.concatenate([w_in[:, :o_kr], kr_pad], axis=1))
            wqb = bf(_pad_heads(mla_w_qb[e], MLA_HEADS, MLA_QK_DIM))
            wkv = mla_w_kvb[e].reshape(MLA_KV_RANK, MLA_HEADS, MLA_NOPE_DIM + MLA_V_DIM)
            wkn = bf(_pad_heads(wkv[:, :, :MLA_NOPE_DIM].reshape(MLA_KV_RANK, -1), MLA_HEADS, MLA_NOPE_DIM))
            wvb = bf(wkv[:, :, MLA_NOPE_DIM:].reshape(MLA_KV_RANK, -1))
            pad128 = lambda g: jnp.pad(g, (0, LANES - g.shape[0]))
            gains = _lane_rows([jnp.tile(diff_q_g[e], 4), jnp.tile(diff_k_g[e], 4),
                                jnp.tile(pad128(mla_q_g[e]), 2), jnp.tile(pad128(mla_k_g[e]), 2)])
            qa, ka, va, qb, kb, vb = _ab_proj(
                xt, mod, w_all, wqb, wkn, wvb, gains, mla_q_lat_g[e].reshape(1, -1), mla_kv_lat_g[e].reshape(1, -1),
                g64, g128, tables, seq)
            lam_rows = _lane_rows([pad128(diff_lam_q1[e]), pad128(diff_lam_k1[e]),
                                   pad128(diff_lam_q2[e]), pad128(diff_lam_k2[e])])
            o_a = _diff_attn(qa, ka, va, lam_rows, diff_subln_g[e].reshape(1, -1), lambda_init, batch, seq)
            o_b = _mla_attn(qb, kb, vb, batch, seq)
            w_out = bf(ab_w_out[e])
            xt = _out_proj(xt, mod, [o_a, o_b], [w_out[:DIFF_W], w_out[DIFF_W:]], seq)
        else:
            o = l // 2
            w_fox = bf(jnp.pad(fox_w_in[o], ((0, 0), (0, LANES - FOX_HEADS))))
            gains = _lane_rows([jnp.tile(fox_q_g[o], 4), jnp.tile(fox_k_g[o], 4)])
            bf_row = jnp.pad(fox_b_f[o], (0, LANES - FOX_HEADS)).reshape(1, LANES)
            tri = jnp.asarray(np.tril(np.ones((ROW_TILE, ROW_TILE), np.float32)), BF16)
            qf, kf, vf, og = _fox_proj(xt, mod, w_fox, gains, bf_row, g64, tri, seq)
            o_f = _fox_attn(qf, kf, vf, og, batch, seq)
            xt = _out_proj(xt, mod, [o_f], [bf(fox_w_out[o])], seq)
        xt = _ffn(xt, mod, 6, bf(ff2_gate[l]), bf(ff2_up[l]), bf(ff2_down[l]), seq)
    return xt.reshape(batch, seq, d)
```

```python
import functools
import math

import numpy as np
import jax
import jax.numpy as jnp
from jax import lax
from jax.experimental import pallas as pl
from jax.experimental.pallas import tpu as pltpu

F32 = jnp.float32
BF16 = jnp.bfloat16

D_MODEL = 1024
D_FF = 2816
N_MOD = 9
ROPE_THETA = 10000.0
EPS = 1e-6
NEG_INF = -1e30
LOG2E = 1.4426950408889634

DIFF_HEADS = 8
DIFF_HEAD_DIM = 64
MLA_HEADS = 8
MLA_NOPE_DIM = 64
MLA_ROPE_DIM = 32
MLA_V_DIM = 128
MLA_Q_RANK = 384
MLA_KV_RANK = 256
FOX_HEADS = 16
FOX_HEAD_DIM = 64

DIFF_W = DIFF_HEADS * 2 * DIFF_HEAD_DIM
MLA_QK_DIM = MLA_NOPE_DIM + MLA_ROPE_DIM
FOX_W = FOX_HEADS * FOX_HEAD_DIM

LANES = 128
SLAB = 256
VMEM_LIMIT = 56 * 1024 * 1024

TILE = 512
FF_CHUNK = 256

FOX_ONES_Q = (64, 65, 66)
FOX_BIAS_Q = (67, 68, 69)


def _params(sem):
    return pltpu.CompilerParams(dimension_semantics=sem, vmem_limit_bytes=VMEM_LIMIT)


def _dot(a, b):
    return jnp.dot(a, b, preferred_element_type=F32)


def _sigmoid(x):
    return 1.0 / (1.0 + jnp.exp(-x))


def _modulated(x, shift, scale):
    ms = jnp.mean(x * x, axis=-1, keepdims=True)
    return x * lax.rsqrt(ms + EPS) * (1.0 + scale) + shift


def _split3(x):
    hi = x.astype(BF16)
    r = x - hi.astype(F32)
    mid = r.astype(BF16)
    lo = (r - mid.astype(F32)).astype(BF16)
    return hi, mid, lo


def _group_sums(sq, gmat):
    hi = sq.astype(BF16)
    lo = (sq - hi.astype(F32)).astype(BF16)
    return _dot(hi, gmat) + _dot(lo, gmat)


def _rope(t, cos, sin_signed, half):
    n = t.shape[-1]
    lane = lax.broadcasted_iota(jnp.int32, t.shape, 1)
    up = pltpu.roll(t, n - half, 1)
    dn = pltpu.roll(t, half, 1)
    partner = jnp.where((lane & half) == 0, up, dn)
    return t * cos + partner * sin_signed


def _put_transposed(ref, first_head, t):
    for j in range(t.shape[1] // LANES):
        ref[0, first_head + j, 0] = t[:, j * LANES:(j + 1) * LANES].T.astype(BF16)


def _adaln_kernel(c_ref, w_ref, b_ref, o_ref):
    c = c_ref[...]
    cond = (c * _sigmoid(c)).astype(BF16)
    o_ref[0] = _dot(cond, w_ref[0].astype(BF16)) + b_ref[0]


def _adaln(c_pad, ada_w, ada_b):
    depth, d, n = ada_w.shape
    tn = 1024
    return pl.pallas_call(
        _adaln_kernel,
        grid=(depth, n // tn),
        in_specs=[
            pl.BlockSpec((8, d), lambda l, j: (0, 0)),
            pl.BlockSpec((1, d, tn), lambda l, j: (l, 0, j)),
            pl.BlockSpec((1, 1, tn), lambda l, j: (l, 0, j)),
        ],
        out_specs=pl.BlockSpec((1, 8, tn), lambda l, j: (l, 0, j)),
        out_shape=jax.ShapeDtypeStruct((depth, 8, n), F32),
        compiler_params=_params(("parallel", "parallel")),
        name="adaln",
    )(c_pad, ada_w, ada_b.reshape(depth, 1, n))


def _rope_table_kernel(pos_ref, inv_ref, sgn_ref, cos_a_ref, sin_a_ref, cos_b_ref, sin_b_ref):
    ang = pos_ref[...] * inv_ref[0:1, :]
    c = jnp.cos(ang)
    s = jnp.sin(ang)
    lane = lax.broadcasted_iota(jnp.int32, ang.shape, 1)
    in_a = lane < 32
    in_b = (lane >= 64) & (lane < 80)
    rope_b = (lane >= 64) & (lane < 96)

    def spread_a(t):
        t = jnp.where(in_a, t, 0.0)
        t = t + pltpu.roll(t, 32, 1)
        return t + pltpu.roll(t, 64, 1)

    def spread_b(t):
        t = jnp.where(in_b, t, 0.0)
        return t + pltpu.roll(t, 16, 1)

    cos_a_ref[...] = spread_a(c)
    sin_a_ref[...] = spread_a(s) * sgn_ref[0:1, :]
    cos_b_ref[...] = jnp.where(rope_b, spread_b(c), 1.0)
    sin_b_ref[...] = spread_b(s) * sgn_ref[1:2, :]


def _rope_tables(pos_b, inv_rows, sgn_rows):
    t = pos_b.shape[0]
    ts = 1024
    spec = pl.BlockSpec((ts, LANES), lambda i: (i, 0))
    cspec = pl.BlockSpec((8, LANES), lambda i: (0, 0))
    out = jax.ShapeDtypeStruct((t, LANES), F32)
    return pl.pallas_call(
        _rope_table_kernel,
        grid=(t // ts,),
        in_specs=[spec, cspec, cspec],
        out_specs=[spec, spec, spec, spec],
        out_shape=[out, out, out, out],
        compiler_params=_params(("parallel",)),
        name="rope_tables",
    )(pos_b, inv_rows, sgn_rows)


def _ffn_kernel(x_ref, sh_ref, sc_ref, g_ref, wg_ref, wu_ref, wd_ref, o_ref):
    x = x_ref[...]
    h = _modulated(x, sh_ref[0], sc_ref[0]).astype(BF16)
    acc = jnp.zeros(x.shape, F32)
    for c in range(D_FF // FF_CHUNK):
        cols = slice(c * FF_CHUNK, (c + 1) * FF_CHUNK)
        g = _dot(h, wg_ref[:, cols])
        u = _dot(h, wu_ref[:, cols])
        a = (g * _sigmoid(g) * u).astype(BF16)
        acc = acc + _dot(a, wd_ref[cols, :])
    o_ref[...] = x + (0.5 * g_ref[0]) * acc


def _mod_spec(seq, which):
    return pl.BlockSpec((1, 1, D_MODEL), lambda i: ((i * TILE // seq) * N_MOD + which, 0, 0))


def _resident(shape):
    return pl.BlockSpec(shape, lambda i: (0,) * len(shape), pipeline_mode=pl.Buffered(1))


def _row_spec(width):
    return pl.BlockSpec((TILE, width), lambda i: (i, 0))


def _ffn(x, mod, which, wg, wu, wd, seq):
    t = x.shape[0]
    row = _row_spec(D_MODEL)
    return pl.pallas_call(
        _ffn_kernel,
        grid=(t // TILE,),
        in_specs=[row, _mod_spec(seq, which), _mod_spec(seq, which + 1), _mod_spec(seq, which + 2),
                  _resident(wg.shape), _resident(wu.shape), _resident(wd.shape)],
        out_specs=row,
        out_shape=jax.ShapeDtypeStruct(x.shape, F32),
        compiler_params=_params(("parallel",)),
        name="ffn",
    )(x, mod, mod, mod, wg, wu, wd)


def _transposed_shape(batch, heads, seq):
    return jax.ShapeDtypeStruct((batch, heads, seq // TILE, LANES, TILE), BF16)


def _transposed_out_spec(heads, seq):
    chunks = seq // TILE
    return pl.BlockSpec((1, heads, 1, LANES, TILE), lambda i: (i // chunks, 0, i % chunks, 0, 0))


def _ab_proj_kernel(x_ref, sh_ref, sc_ref, w_ref, wqb_ref, wkn_ref, wvb_ref, gains_ref, lat_q_ref, lat_kv_ref,
                    g64_ref, g128_ref, cos_a_ref, sin_a_ref, cos_b_ref, sin_b_ref,
                    qat_ref, ka_ref, vat_ref, qbt_ref, kb_ref, vbt_ref):
    x = x_ref[...]
    h = _modulated(x, sh_ref[0], sc_ref[0]).astype(BF16)
    y = _dot(h, w_ref[...])
    two = lambda r: jnp.concatenate([r, r], axis=1)
    cos_a, sin_a = two(cos_a_ref[...]), two(sin_a_ref[...])
    cos_b, sin_b = two(cos_b_ref[...]), two(sin_b_ref[...])
    gq_a, gk_a = gains_ref[0:1, :], gains_ref[1:2, :]
    gq_b, gk_b = gains_ref[2:3, :], gains_ref[3:4, :]
    g64, g128 = g64_ref[...], g128_ref[...]
    scale_a = DIFF_HEAD_DIM ** -0.5 * LOG2E
    scale_b = MLA_QK_DIM ** -0.5 * LOG2E
    heads_per_slab = SLAB // LANES

    def head_norm(t, gmat, dim, gain):
        ss = _group_sums(t * t, gmat)
        return t * lax.rsqrt(ss * (1.0 / dim) + EPS) * gain

    for s in range(DIFF_W // SLAB):
        cols = slice(s * SLAB, (s + 1) * SLAB)
        q = _rope(head_norm(y[:, cols], g64, DIFF_HEAD_DIM, gq_a), cos_a, sin_a, 32)
        _put_transposed(qat_ref, heads_per_slab * s, q * scale_a)
        kcols = slice(DIFF_W + s * SLAB, DIFF_W + (s + 1) * SLAB)
        k = _rope(head_norm(y[:, kcols], g64, DIFF_HEAD_DIM, gk_a), cos_a, sin_a, 32)
        ka_ref[:, cols] = k.astype(BF16)
    _put_transposed(vat_ref, 0, y[:, 2 * DIFF_W:3 * DIFF_W])

    o_cq = 3 * DIFF_W
    o_ckv = o_cq + MLA_Q_RANK
    o_kr = o_ckv + MLA_KV_RANK
    c_q = y[:, o_cq:o_ckv]
    c_q = c_q * lax.rsqrt(jnp.mean(c_q * c_q, axis=-1, keepdims=True) + EPS) * lat_q_ref[...]
    q_b = _dot(c_q.astype(BF16), wqb_ref[...])
    c_kv = y[:, o_ckv:o_kr]
    c_kv = (c_kv * lax.rsqrt(jnp.mean(c_kv * c_kv, axis=-1, keepdims=True) + EPS) * lat_kv_ref[...]).astype(BF16)
    k_nope = _dot(c_kv, wkn_ref[...])
    _put_transposed(vbt_ref, 0, _dot(c_kv, wvb_ref[...]))
    k_rope = two(y[:, o_kr:o_kr + LANES])
    for s in range(MLA_HEADS * LANES // SLAB):
        cols = slice(s * SLAB, (s + 1) * SLAB)
        q = _rope(head_norm(q_b[:, cols], g128, MLA_QK_DIM, gq_b), cos_b, sin_b, 16)
        _put_transposed(qbt_ref, heads_per_slab * s, q * scale_b)
        k = _rope(head_norm(k_nope[:, cols] + k_rope, g128, MLA_QK_DIM, gk_b), cos_b, sin_b, 16)
        kb_ref[:, cols] = k.astype(BF16)


def _ab_proj(x, mod, w_all, wqb, wkn, wvb, gains, lat_q, lat_kv, g64, g128, tables, batch, seq):
    t = x.shape[0]
    row = _row_spec(D_MODEL)
    tab = _row_spec(LANES)
    rows_out = jax.ShapeDtypeStruct((t, D_MODEL), BF16)
    tr_out = _transposed_shape(batch, DIFF_HEADS, seq)
    tr_spec = _transposed_out_spec(DIFF_HEADS, seq)
    return pl.pallas_call(
        _ab_proj_kernel,
        grid=(t // TILE,),
        in_specs=[row, _mod_spec(seq, 3), _mod_spec(seq, 4),
                  _resident(w_all.shape), _resident(wqb.shape), _resident(wkn.shape), _resident(wvb.shape),
                  _resident(gains.shape), _resident(lat_q.shape), _resident(lat_kv.shape),
                  _resident(g64.shape), _resident(g128.shape), tab, tab, tab, tab],
        out_specs=[tr_spec, row, tr_spec, tr_spec, row, tr_spec],
        out_shape=[tr_out, rows_out, tr_out, tr_out, rows_out, tr_out],
        compiler_params=_params(("parallel",)),
        name="ab_proj",
    )(x, mod, mod, w_all, wqb, wkn, wvb, gains, lat_q, lat_kv, g64, g128, *tables)


def _fox_proj_kernel(seq, x_ref, sh_ref, sc_ref, w_ref, gains_ref, bf_ref, g64_ref, tri_ref,
                     qft_ref, kf_ref, vft_ref, og_ref, carry_ref):
    i = pl.program_id(0)
    tm = x_ref.shape[0]

    @pl.when((i * tm) % seq == 0)
    def _():
        carry_ref[...] = jnp.zeros(carry_ref.shape, F32)

    x = x_ref[...]
    h = _modulated(x, sh_ref[0], sc_ref[0]).astype(BF16)
    y = _dot(h, w_ref[...])
    _put_transposed(vft_ref, 0, y[:, 2 * FOX_W:3 * FOX_W])
    og_ref[...] = y[:, 3 * FOX_W:4 * FOX_W]

    lane = lax.broadcasted_iota(jnp.int32, (tm, LANES), 1)
    z = y[:, 4 * FOX_W:4 * FOX_W + LANES] + bf_ref[...]
    log_f = jnp.where(lane < FOX_HEADS, jnp.minimum(z, 0.0) - jnp.log1p(jnp.exp(-jnp.abs(z))), 0.0)
    tri = tri_ref[...]
    hi, mid, lo = _split3(log_f)
    cum = _dot(tri, hi) + _dot(tri, mid) + _dot(tri, lo) + carry_ref[0:1, :]
    carry_ref[0:1, :] = cum[tm - 1:tm, :]
    cum = cum * LOG2E

    gq, gk = gains_ref[0:1, :], gains_ref[1:2, :]
    g64 = g64_ref[...]
    scale = FOX_HEAD_DIM ** -0.5 * LOG2E

    def head_norm(t, gain):
        ss = _group_sums(t * t, g64)
        return t * lax.rsqrt(ss * (1.0 / FOX_HEAD_DIM) + EPS) * gain

    def bias_lanes(pieces, ones_at, pieces_at, sign):
        blk = jnp.zeros((tm, LANES), F32)
        for ln in ones_at:
            blk = jnp.where(lane == ln, 1.0, blk)
        for ln, p in zip(pieces_at, pieces):
            blk = jnp.where(lane == ln, sign * p.astype(F32), blk)
        return blk

    for s in range(FOX_W // SLAB):
        cols = slice(s * SLAB, (s + 1) * SLAB)
        qn = head_norm(y[:, cols], gq) * scale
        kn = head_norm(y[:, FOX_W + s * SLAB:FOX_W + (s + 1) * SLAB], gk)
        for j in range(SLAB // FOX_HEAD_DIM):
            hd = s * (SLAB // FOX_HEAD_DIM) + j
            blk = slice((j // 2) * LANES, (j // 2 + 1) * LANES)
            qh, kh = qn[:, blk], kn[:, blk]
            if j % 2:
                qh, kh = pltpu.roll(qh, FOX_HEAD_DIM, 1), pltpu.roll(kh, FOX_HEAD_DIM, 1)
            pieces = _split3(jnp.broadcast_to(cum[:, hd:hd + 1], (tm, LANES)))
            q_aug = bias_lanes(pieces, FOX_ONES_Q, FOX_BIAS_Q, 1.0)
            k_aug = bias_lanes(pieces, FOX_BIAS_Q, FOX_ONES_Q, -1.0)
            _put_transposed(qft_ref, hd, jnp.where(lane < FOX_HEAD_DIM, qh, q_aug))
            kf_ref[:, hd * LANES:(hd + 1) * LANES] = jnp.where(lane < FOX_HEAD_DIM, kh, k_aug).astype(BF16)


def _fox_proj(x, mod, w_fox, gains, bf_row, g64, tri, batch, seq):
    t = x.shape[0]
    row = _row_spec(D_MODEL)
    return pl.pallas_call(
        functools.partial(_fox_proj_kernel, seq),
        grid=(t // TILE,),
        in_specs=[row, _mod_spec(seq, 3), _mod_spec(seq, 4), _resident(w_fox.shape),
                  _resident(gains.shape), _resident(bf_row.shape), _resident(g64.shape), _resident(tri.shape)],
        out_specs=[_transposed_out_spec(FOX_HEADS, seq), _row_spec(FOX_HEADS * LANES),
                   _transposed_out_spec(FOX_HEADS // 2, seq), row],
        out_shape=[_transposed_shape(batch, FOX_HEADS, seq), jax.ShapeDtypeStruct((t, FOX_HEADS * LANES), BF16),
                   _transposed_shape(batch, FOX_HEADS // 2, seq), jax.ShapeDtypeStruct((t, FOX_W), F32)],
        scratch_shapes=[pltpu.VMEM((8, LANES), F32)],
        compiler_params=_params(("arbitrary",)),
        name="fox_proj",
    )(x, mod, mod, w_fox, gains, bf_row, g64, tri)


def _flash_t(qts, k_refs, vt_at, qi, s_buf):
    n = len(qts)

    def issue(j, slot, diagonal):
        off = pl.multiple_of(j * TILE, TILE)
        maxima = []
        for e in range(n):
            s = _dot(k_refs[e][pl.ds(off, TILE), :], qts[e])
            if diagonal:
                key = lax.broadcasted_iota(jnp.int32, (TILE, TILE), 0)
                query = lax.broadcasted_iota(jnp.int32, (TILE, TILE), 1)
                s = jnp.where(key <= query, s, NEG_INF)
            s_buf[slot, e] = s
            maxima.append(jnp.max(s, axis=0, keepdims=True))
        return tuple(maxima)

    def absorb(j, slot, maxima, stats):
        new = []
        for e in range(n):
            m, l, acc = stats[e]
            m_new = jnp.maximum(m, maxima[e])
            alpha = jnp.exp2(m - m_new)
            p = jnp.exp2(s_buf[slot, e] - m_new)
            l = alpha * l + jnp.sum(p, axis=0, keepdims=True)
            acc = alpha * acc + _dot(vt_at[e](j), p.astype(BF16))
            new.append((m_new, l, acc))
        return tuple(new)

    def body(jj, carry):
        max_a, stats = carry
        held = jnp.where(jj == 0, qi, 2 * jj - 1)
        max_b = issue(2 * jj, 1, False)
        stats = absorb(held, 0, max_a, stats)
        max_a = issue(2 * jj + 1, 0, False)
        stats = absorb(2 * jj, 1, max_b, stats)
        return max_a, stats

    init = tuple((jnp.full((1, TILE), NEG_INF, F32), jnp.zeros((1, TILE), F32), jnp.zeros((LANES, TILE), F32))
                 for _ in range(n))
    pairs = qi // 2
    max_a, stats = lax.fori_loop(0, pairs, body, (issue(qi, 0, True), init))
    held = jnp.where(pairs == 0, qi, 2 * pairs - 1)

    def odd_tail(stats):
        max_b = issue(qi - 1, 1, False)
        return absorb(qi - 1, 1, max_b, absorb(held, 0, max_a, stats))

    stats = lax.cond(qi % 2 == 1, odd_tail, lambda st: absorb(held, 0, max_a, st), stats)
    return [acc / l for (_, l, acc) in stats]


def _diff_attn_kernel(lambda_init, qt_ref, k_ref, vt_ref, lam_ref, subln_ref, o_ref, s_buf):
    qt = qt_ref[0, 0, 0]
    feat = lax.broadcasted_iota(jnp.int32, qt.shape, 0)
    zero = jnp.zeros_like(qt)
    q1 = jnp.where(feat < DIFF_HEAD_DIM, qt, zero)
    q2 = jnp.where(feat < DIFF_HEAD_DIM, zero, qt)
    vt = lambda j: vt_ref[0, 0, j]
    o1, o2 = _flash_t([q1, q2], [k_ref, k_ref], [vt, vt], pl.program_id(2), s_buf)
    lv = lam_ref[...]
    lam = (jnp.exp(jnp.sum(lv[0:1] * lv[1:2], axis=-1, keepdims=True))
           - jnp.exp(jnp.sum(lv[2:3] * lv[3:4], axis=-1, keepdims=True)) + lambda_init)
    o = (o1 - lam * o2).T
    o = o * lax.rsqrt(jnp.mean(o * o, axis=-1, keepdims=True) + EPS) * subln_ref[...]
    o_ref[...] = (o * (1.0 - lambda_init)).astype(BF16)


def _mla_attn_kernel(qt_ref, k0_ref, k1_ref, vt_ref, o_ref, s_buf):
    outs = _flash_t([qt_ref[0, 0, 0], qt_ref[0, 1, 0]], [k0_ref, k1_ref],
                    [lambda j: vt_ref[0, 0, j], lambda j: vt_ref[0, 1, j]], pl.program_id(2), s_buf)
    for e, o in enumerate(outs):
        o_ref[:, e * LANES:(e + 1) * LANES] = o.T.astype(BF16)


def _fox_attn_kernel(qt_ref, k0_ref, k1_ref, vt_ref, og_ref, o_ref, s_buf):
    vt = lambda j: vt_ref[0, 0, j]
    o1, o2 = _flash_t([qt_ref[0, 0, 0], qt_ref[0, 1, 0]], [k0_ref, k1_ref], [vt, vt], pl.program_id(2), s_buf)
    feat = lax.broadcasted_iota(jnp.int32, o1.shape, 0)
    o = jnp.where(feat < FOX_HEAD_DIM, o1, o2).T
    o_ref[...] = (o * _sigmoid(og_ref[...])).astype(BF16)


def _qt_spec(heads):
    return pl.BlockSpec((1, heads, 1, LANES, TILE), lambda b, h, i: (b, h, i, 0, 0))


def _vt_spec(heads, seq):
    return pl.BlockSpec((1, heads, seq // TILE, LANES, TILE), lambda b, h, i: (b, h, 0, 0, 0))


def _k_spec(seq, head_of):
    return pl.BlockSpec((seq, LANES), lambda b, h, i: (b, head_of(h)))


def _o_spec(seq, width):
    nq = seq // TILE
    return pl.BlockSpec((TILE, width), lambda b, h, i: (b * nq + i, h))


_ATT_SEMANTICS = ("parallel", "parallel", "arbitrary")
_SCORE_SCRATCH = pltpu.VMEM((2, 2, TILE, TILE), F32)


def _diff_attn(qat, ka, vat, lam_rows, subln, lambda_init, batch, seq):
    const = lambda shape: pl.BlockSpec(shape, lambda b, h, i: (0, 0))
    return pl.pallas_call(
        functools.partial(_diff_attn_kernel, lambda_init),
        grid=(batch, DIFF_HEADS, seq // TILE),
        in_specs=[_qt_spec(1), _k_spec(seq, lambda h: h), _vt_spec(1, seq), const(lam_rows.shape), const(subln.shape)],
        out_specs=_o_spec(seq, LANES),
        out_shape=jax.ShapeDtypeStruct(ka.shape, BF16),
        scratch_shapes=[_SCORE_SCRATCH],
        compiler_params=_params(_ATT_SEMANTICS),
        name="diff_attn",
    )(qat, ka, vat, lam_rows, subln)


def _mla_attn(qbt, kb, vbt, batch, seq):
    return pl.pallas_call(
        _mla_attn_kernel,
        grid=(batch, MLA_HEADS // 2, seq // TILE),
        in_specs=[_qt_spec(2), _k_spec(seq, lambda p: 2 * p), _k_spec(seq, lambda p: 2 * p + 1), _vt_spec(2, seq)],
        out_specs=_o_spec(seq, 2 * LANES),
        out_shape=jax.ShapeDtypeStruct(kb.shape, BF16),
        scratch_shapes=[_SCORE_SCRATCH],
        compiler_params=_params(_ATT_SEMANTICS),
        name="mla_attn",
    )(qbt, kb, kb, vbt)


def _fox_attn(qft, kf, vft, og, batch, seq):
    return pl.pallas_call(
        _fox_attn_kernel,
        grid=(batch, FOX_HEADS // 2, seq // TILE),
        in_specs=[_qt_spec(2), _k_spec(seq, lambda p: 2 * p), _k_spec(seq, lambda p: 2 * p + 1), _vt_spec(1, seq),
                  _o_spec(seq, LANES)],
        out_specs=_o_spec(seq, LANES),
        out_shape=jax.ShapeDtypeStruct(og.shape, BF16),
        scratch_shapes=[_SCORE_SCRATCH],
        compiler_params=_params(_ATT_SEMANTICS),
        name="fox_attn",
    )(qft, kf, kf, vft, og)


def _out_proj_kernel(n_in, *refs):
    x_ref, g_ref = refs[0], refs[1]
    o_refs = refs[2:2 + n_in]
    w_refs = refs[2 + n_in:2 + 2 * n_in]
    out_ref = refs[2 + 2 * n_in]
    mix = _dot(o_refs[0][...], w_refs[0][...])
    for o_ref, w_ref in zip(o_refs[1:], w_refs[1:]):
        mix = mix + _dot(o_ref[...], w_ref[...])
    out_ref[...] = x_ref[...] + g_ref[0] * mix


def _out_proj(x, mod, os, ws, seq):
    t = x.shape[0]
    row = _row_spec(D_MODEL)
    return pl.pallas_call(
        functools.partial(_out_proj_kernel, len(os)),
        grid=(t // TILE,),
        in_specs=[row, _mod_spec(seq, 5)] + [_row_spec(o.shape[1]) for o in os] + [_resident(w.shape) for w in ws],
        out_specs=row,
        out_shape=jax.ShapeDtypeStruct(x.shape, F32),
        compiler_params=_params(("parallel",)),
        name="out_proj",
    )(x, mod, *os, *ws)


def _lane_rows(rows):
    width = max(r.shape[0] for r in rows)
    out = jnp.zeros((8, width), F32)
    for i, r in enumerate(rows):
        out = out.at[i, :r.shape[0]].set(r.astype(F32))
    return out


def _block_diag_ones(n, group):
    idx = np.arange(n) // group
    return jnp.asarray((idx[:, None] == idx[None, :]).astype(np.float32), BF16)


def _pad_heads(w, heads, dim):
    r = w.shape[0]
    w = w.reshape(r, heads, dim)
    return jnp.pad(w, ((0, 0), (0, 0), (0, LANES - dim))).reshape(r, heads * LANES)


def kernel(x, c, positions, ada_w, ada_b, ff1_gate, ff1_up, ff1_down, ff2_gate, ff2_up, ff2_down, ab_w_in, mla_w_qb, mla_w_kvb, mla_q_lat_g, mla_kv_lat_g, diff_q_g, diff_k_g, mla_q_g, mla_k_g, diff_lam_q1, diff_lam_k1, diff_lam_q2, diff_lam_k2, diff_subln_g, ab_w_out, fox_w_in, fox_b_f, fox_q_g, fox_k_g, fox_w_out):
    batch, seq, d = x.shape
    depth = ada_w.shape[0]
    t = batch * seq
    assert d == D_MODEL and seq % TILE == 0 and batch <= 8

    c_pad = jnp.zeros((8, d), F32).at[:batch].set(c)
    mod_all = _adaln(c_pad, ada_w, ada_b)

    inv_a = ROPE_THETA ** (-jnp.arange(0, DIFF_HEAD_DIM, 2, dtype=F32) / DIFF_HEAD_DIM)
    inv_b = ROPE_THETA ** (-jnp.arange(0, MLA_ROPE_DIM, 2, dtype=F32) / MLA_ROPE_DIM)
    inv_rows = jnp.zeros((8, LANES), F32).at[0, :32].set(inv_a).at[0, 64:80].set(inv_b)
    lane = np.arange(LANES)
    sgn_a = np.where(lane % 64 < 32, -1.0, 1.0)
    sgn_b = np.where((lane >= 64) & (lane < 80), -1.0, np.where((lane >= 80) & (lane < 96), 1.0, 0.0))
    sgn_rows = jnp.asarray(np.stack([sgn_a, sgn_b] + [np.zeros(LANES)] * 6), F32)
    pos_b = jnp.broadcast_to(positions.astype(F32).reshape(t, 1), (t, LANES))
    tables = _rope_tables(pos_b, inv_rows, sgn_rows)

    g64 = _block_diag_ones(SLAB, 64)
    g128 = _block_diag_ones(SLAB, LANES)
    bf = lambda w: w.astype(BF16)

    xt = x.reshape(t, d)
    for l in range(depth):
        mod = mod_all[l, :batch].reshape(batch * N_MOD, 1, d)
        xt = _ffn(xt, mod, 0, bf(ff1_gate[l]), bf(ff1_up[l]), bf(ff1_down[l]), seq)
        if l % 2 == 0:
            e = l // 2
            lambda_init = 0.8 - 0.6 * math.exp(-0.3 * l)
            w_in = ab_w_in[e]
            o_cq = 3 * DIFF_W
            o_kr = o_cq + MLA_Q_RANK + MLA_KV_RANK
            kr_pad = jnp.zeros((d, LANES), F32).at[:, MLA_NOPE_DIM:MLA_QK_DIM].set(w_in[:, o_kr:])
            w_all = bf(jnp.concatenate([w_in[:, :o_kr], kr_pad], axis=1))
            wqb = bf(_pad_heads(mla_w_qb[e], MLA_HEADS, MLA_QK_DIM))
            wkv = mla_w_kvb[e].reshape(MLA_KV_RANK, MLA_HEADS, MLA_NOPE_DIM + MLA_V_DIM)
            wkn = bf(_pad_heads(wkv[:, :, :MLA_NOPE_DIM].reshape(MLA_KV_RANK, -1), MLA_HEADS, MLA_NOPE_DIM))
            wvb = bf(wkv[:, :, MLA_NOPE_DIM:].reshape(MLA_KV_RANK, -1))
            pad128 = lambda g: jnp.pad(g, (0, LANES - g.shape[0]))
            gains = _lane_rows([jnp.tile(diff_q_g[e], 4), jnp.tile(diff_k_g[e], 4),
                                jnp.tile(pad128(mla_q_g[e]), 2), jnp.tile(pad128(mla_k_g[e]), 2)])
            qat, ka, vat, qbt, kb, vbt = _ab_proj(
                xt, mod, w_all, wqb, wkn, wvb, gains, mla_q_lat_g[e].reshape(1, -1), mla_kv_lat_g[e].reshape(1, -1),
                g64, g128, tables, batch, seq)
            lam_rows = _lane_rows([pad128(diff_lam_q1[e]), pad128(diff_lam_k1[e]),
                                   pad128(diff_lam_q2[e]), pad128(diff_lam_k2[e])])
            o_a = _diff_attn(qat, ka, vat, lam_rows, diff_subln_g[e].reshape(1, -1), lambda_init, batch, seq)
            o_b = _mla_attn(qbt, kb, vbt, batch, seq)
            w_out = bf(ab_w_out[e])
            xt = _out_proj(xt, mod, [o_a, o_b], [w_out[:DIFF_W], w_out[DIFF_W:]], seq)
        else:
            o = l // 2
            w_fox = bf(jnp.pad(fox_w_in[o], ((0, 0), (0, LANES - FOX_HEADS))))
            gains = _lane_rows([jnp.tile(fox_q_g[o], 4), jnp.tile(fox_k_g[o], 4)])
            bf_row = jnp.pad(fox_b_f[o], (0, LANES - FOX_HEADS)).reshape(1, LANES)
            tri = jnp.asarray(np.tril(np.ones((TILE, TILE), np.float32)), BF16)
            qft, kf, vft, og = _fox_proj(xt, mod, w_fox, gains, bf_row, g64, tri, batch, seq)
            o_f = _fox_attn(qft, kf, vft, og, batch, seq)
            xt = _out_proj(xt, mod, [o_f], [bf(fox_w_out[o])], seq)
        xt = _ffn(xt, mod, 6, bf(ff2_gate[l]), bf(ff2_up[l]), bf(ff2_down[l]), seq)
    return xt.reshape(batch, seq, d)
```

```python
import functools
import math

import numpy as np
import jax
import jax.numpy as jnp
from jax import lax
from jax.experimental import pallas as pl
from jax.experimental.pallas import tpu as pltpu

F32 = jnp.float32
BF16 = jnp.bfloat16

D_MODEL = 1024
D_FF = 2816
N_MOD = 9
ROPE_THETA = 10000.0
EPS = 1e-6
NEG_INF = -1e30
LOG2E = 1.4426950408889634

DIFF_HEADS = 8
DIFF_HEAD_DIM = 64
MLA_HEADS = 8
MLA_NOPE_DIM = 64
MLA_ROPE_DIM = 32
MLA_V_DIM = 128
MLA_Q_RANK = 384
MLA_KV_RANK = 256
FOX_HEADS = 16
FOX_HEAD_DIM = 64

DIFF_W = DIFF_HEADS * 2 * DIFF_HEAD_DIM
MLA_QK_DIM = MLA_NOPE_DIM + MLA_ROPE_DIM
FOX_W = FOX_HEADS * FOX_HEAD_DIM

LANES = 128
SLAB = 256
VMEM_LIMIT = 56 * 1024 * 1024

TILE = 512
FF_CHUNK = 256

FOX_ONES_Q = (64, 65, 66)
FOX_BIAS_Q = (67, 68, 69)


def _params(sem):
    return pltpu.CompilerParams(dimension_semantics=sem, vmem_limit_bytes=VMEM_LIMIT)


def _dot(a, b):
    return jnp.dot(a, b, preferred_element_type=F32)


def _sigmoid(x):
    return 1.0 / (1.0 + jnp.exp(-x))


def _modulated(x, shift, scale):
    ms = jnp.mean(x * x, axis=-1, keepdims=True)
    return x * lax.rsqrt(ms + EPS) * (1.0 + scale) + shift


def _split3(x):
    hi = x.astype(BF16)
    r = x - hi.astype(F32)
    mid = r.astype(BF16)
    lo = (r - mid.astype(F32)).astype(BF16)
    return hi, mid, lo


def _group_sums(sq, gmat):
    return _dot(sq.astype(BF16), gmat)


def _rope(t, cos, sin_signed, half):
    n = t.shape[-1]
    lane = lax.broadcasted_iota(jnp.int32, t.shape, 1)
    up = pltpu.roll(t, n - half, 1)
    dn = pltpu.roll(t, half, 1)
    partner = jnp.where((lane & half) == 0, up, dn)
    return t * cos + partner * sin_signed


def _put_transposed(ref, first_head, t):
    for j in range(t.shape[1] // LANES):
        ref[0, first_head + j, 0] = t[:, j * LANES:(j + 1) * LANES].T.astype(BF16)


def _cast_kernel(w_ref, o_ref):
    o_ref[...] = w_ref[...].astype(BF16)


def _to_bf16(w):
    depth, rows, cols = w.shape
    rb = 256
    spec = pl.BlockSpec((1, rb, cols), lambda l, i: (l, i, 0))
    return pl.pallas_call(
        _cast_kernel,
        grid=(depth, rows // rb),
        in_specs=[spec],
        out_specs=spec,
        out_shape=jax.ShapeDtypeStruct(w.shape, BF16),
        compiler_params=_params(("parallel", "parallel")),
        name="to_bf16",
    )(w)


def _adaln_kernel(c_ref, w_ref, b_ref, o_ref):
    c = c_ref[...]
    cond = (c * _sigmoid(c)).astype(BF16)
    o_ref[0] = _dot(cond, w_ref[0].astype(BF16)) + b_ref[0]


def _adaln(c_pad, ada_w, ada_b):
    depth, d, n = ada_w.shape
    tn = 1024
    return pl.pallas_call(
        _adaln_kernel,
        grid=(depth, n // tn),
        in_specs=[
            pl.BlockSpec((8, d), lambda l, j: (0, 0)),
            pl.BlockSpec((1, d, tn), lambda l, j: (l, 0, j)),
            pl.BlockSpec((1, 1, tn), lambda l, j: (l, 0, j)),
        ],
        out_specs=pl.BlockSpec((1, 8, tn), lambda l, j: (l, 0, j)),
        out_shape=jax.ShapeDtypeStruct((depth, 8, n), F32),
        compiler_params=_params(("parallel", "parallel")),
        name="adaln",
    )(c_pad, ada_w, ada_b.reshape(depth, 1, n))


def _rope_table_kernel(pos_ref, inv_ref, sgn_ref, cos_a_ref, sin_a_ref, cos_b_ref, sin_b_ref):
    ang = pos_ref[...] * inv_ref[0:1, :]
    c = jnp.cos(ang)
    s = jnp.sin(ang)
    lane = lax.broadcasted_iota(jnp.int32, ang.shape, 1)
    in_a = lane < 32
    in_b = (lane >= 64) & (lane < 80)
    rope_b = (lane >= 64) & (lane < 96)

    def spread_a(t):
        t = jnp.where(in_a, t, 0.0)
        t = t + pltpu.roll(t, 32, 1)
        return t + pltpu.roll(t, 64, 1)

    def spread_b(t):
        t = jnp.where(in_b, t, 0.0)
        return t + pltpu.roll(t, 16, 1)

    cos_a_ref[...] = spread_a(c)
    sin_a_ref[...] = spread_a(s) * sgn_ref[0:1, :]
    cos_b_ref[...] = jnp.where(rope_b, spread_b(c), 1.0)
    sin_b_ref[...] = spread_b(s) * sgn_ref[1:2, :]


def _rope_tables(pos_b, inv_rows, sgn_rows):
    t = pos_b.shape[0]
    ts = TILE
    spec = pl.BlockSpec((ts, LANES), lambda i: (i, 0))
    cspec = pl.BlockSpec((8, LANES), lambda i: (0, 0))
    out = jax.ShapeDtypeStruct((t, LANES), F32)
    return pl.pallas_call(
        _rope_table_kernel,
        grid=(t // ts,),
        in_specs=[spec, cspec, cspec],
        out_specs=[spec, spec, spec, spec],
        out_shape=[out, out, out, out],
        compiler_params=_params(("parallel",)),
        name="rope_tables",
    )(pos_b, inv_rows, sgn_rows)


def _ffn_kernel(x_ref, sh_ref, sc_ref, g_ref, wg_ref, wu_ref, wd_ref, o_ref):
    x = x_ref[...]
    h = _modulated(x, sh_ref[0], sc_ref[0]).astype(BF16)
    acc = jnp.zeros(x.shape, F32)
    for c in range(D_FF // FF_CHUNK):
        cols = slice(c * FF_CHUNK, (c + 1) * FF_CHUNK)
        g = _dot(h, wg_ref[0, :, cols])
        u = _dot(h, wu_ref[0, :, cols])
        a = (g * _sigmoid(g) * u).astype(BF16)
        acc = acc + _dot(a, wd_ref[0, cols, :])
    o_ref[...] = x + (0.5 * g_ref[0]) * acc


def _mod_spec(seq, which):
    return pl.BlockSpec((1, 1, D_MODEL), lambda i: ((i * TILE // seq) * N_MOD + which, 0, 0))


def _resident(shape):
    return pl.BlockSpec(shape, lambda i: (0,) * len(shape), pipeline_mode=pl.Buffered(1))


def _row_spec(width):
    return pl.BlockSpec((TILE, width), lambda i: (i, 0))


def _ffn(x, mod, which, layer, wg, wu, wd, seq):
    t = x.shape[0]
    row = _row_spec(D_MODEL)
    of_layer = lambda w: pl.BlockSpec((1,) + w.shape[1:], lambda i: (layer, 0, 0), pipeline_mode=pl.Buffered(1))
    return pl.pallas_call(
        _ffn_kernel,
        grid=(t // TILE,),
        in_specs=[row, _mod_spec(seq, which), _mod_spec(seq, which + 1), _mod_spec(seq, which + 2),
                  of_layer(wg), of_layer(wu), of_layer(wd)],
        out_specs=row,
        out_shape=jax.ShapeDtypeStruct(x.shape, F32),
        compiler_params=_params(("parallel",)),
        name="ffn",
    )(x, mod, mod, mod, wg, wu, wd)


def _transposed_shape(batch, heads, seq):
    return jax.ShapeDtypeStruct((batch, heads, seq // TILE, LANES, TILE), BF16)


def _transposed_out_spec(heads, seq):
    chunks = seq // TILE
    return pl.BlockSpec((1, heads, 1, LANES, TILE), lambda i: (i // chunks, 0, i % chunks, 0, 0))


def _ab_proj_kernel(x_ref, sh_ref, sc_ref, w_ref, wqb_ref, wkn_ref, wvb_ref, gains_ref, lat_q_ref, lat_kv_ref,
                    g64_ref, g128_ref, cos_a_ref, sin_a_ref, cos_b_ref, sin_b_ref,
                    qat_ref, ka_ref, vat_ref, qbt_ref, kb_ref, vbt_ref):
    x = x_ref[...]
    h = _modulated(x, sh_ref[0], sc_ref[0]).astype(BF16)
    y = _dot(h, w_ref[...])
    two = lambda r: jnp.concatenate([r, r], axis=1)
    cos_a, sin_a = two(cos_a_ref[...]), two(sin_a_ref[...])
    cos_b, sin_b = two(cos_b_ref[...]), two(sin_b_ref[...])
    gq_a, gk_a = gains_ref[0:1, :], gains_ref[1:2, :]
    gq_b, gk_b = gains_ref[2:3, :], gains_ref[3:4, :]
    g64, g128 = g64_ref[...], g128_ref[...]
    scale_a = DIFF_HEAD_DIM ** -0.5 * LOG2E
    scale_b = MLA_QK_DIM ** -0.5 * LOG2E
    heads_per_slab = SLAB // LANES

    def head_norm(t, gmat, dim, gain):
        ss = _group_sums(t * t, gmat)
        return t * lax.rsqrt(ss * (1.0 / dim) + EPS) * gain

    for s in range(DIFF_W // SLAB):
        cols = slice(s * SLAB, (s + 1) * SLAB)
        q = _rope(head_norm(y[:, cols], g64, DIFF_HEAD_DIM, gq_a), cos_a, sin_a, 32)
        _put_transposed(qat_ref, heads_per_slab * s, q * scale_a)
        kcols = slice(DIFF_W + s * SLAB, DIFF_W + (s + 1) * SLAB)
        k = _rope(head_norm(y[:, kcols], g64, DIFF_HEAD_DIM, gk_a), cos_a, sin_a, 32)
        ka_ref[:, cols] = k.astype(BF16)
    _put_transposed(vat_ref, 0, y[:, 2 * DIFF_W:3 * DIFF_W])

    o_cq = 3 * DIFF_W
    o_ckv = o_cq + MLA_Q_RANK
    o_kr = o_ckv + MLA_KV_RANK
    c_q = y[:, o_cq:o_ckv]
    c_q = c_q * lax.rsqrt(jnp.mean(c_q * c_q, axis=-1, keepdims=True) + EPS) * lat_q_ref[...]
    q_b = _dot(c_q.astype(BF16), wqb_ref[...])
    c_kv = y[:, o_ckv:o_kr]
    c_kv = (c_kv * lax.rsqrt(jnp.mean(c_kv * c_kv, axis=-1, keepdims=True) + EPS) * lat_kv_ref[...]).astype(BF16)
    k_nope = _dot(c_kv, wkn_ref[...])
    _put_transposed(vbt_ref, 0, _dot(c_kv, wvb_ref[...]))
    k_rope = two(y[:, o_kr:o_kr + LANES])
    for s in range(MLA_HEADS * LANES // SLAB):
        cols = slice(s * SLAB, (s + 1) * SLAB)
        q = _rope(head_norm(q_b[:, cols], g128, MLA_QK_DIM, gq_b), cos_b, sin_b, 16)
        _put_transposed(qbt_ref, heads_per_slab * s, q * scale_b)
        k = _rope(head_norm(k_nope[:, cols] + k_rope, g128, MLA_QK_DIM, gk_b), cos_b, sin_b, 16)
        kb_ref[:, cols] = k.astype(BF16)


def _ab_proj(x, mod, w_all, wqb, wkn, wvb, gains, lat_q, lat_kv, g64, g128, tables, batch, seq):
    t = x.shape[0]
    row = _row_spec(D_MODEL)
    tab = _row_spec(LANES)
    rows_out = jax.ShapeDtypeStruct((t, D_MODEL), BF16)
    tr_out = _transposed_shape(batch, DIFF_HEADS, seq)
    tr_spec = _transposed_out_spec(DIFF_HEADS, seq)
    return pl.pallas_call(
        _ab_proj_kernel,
        grid=(t // TILE,),
        in_specs=[row, _mod_spec(seq, 3), _mod_spec(seq, 4),
                  _resident(w_all.shape), _resident(wqb.shape), _resident(wkn.shape), _resident(wvb.shape),
                  _resident(gains.shape), _resident(lat_q.shape), _resident(lat_kv.shape),
                  _resident(g64.shape), _resident(g128.shape), tab, tab, tab, tab],
        out_specs=[tr_spec, row, tr_spec, tr_spec, row, tr_spec],
        out_shape=[tr_out, rows_out, tr_out, tr_out, rows_out, tr_out],
        compiler_params=_params(("parallel",)),
        name="ab_proj",
    )(x, mod, mod, w_all, wqb, wkn, wvb, gains, lat_q, lat_kv, g64, g128, *tables)


def _fox_proj_kernel(seq, x_ref, sh_ref, sc_ref, w_ref, gains_ref, bf_ref, g64_ref, tri_ref,
                     qft_ref, kf_ref, vft_ref, og_ref, carry_ref):
    i = pl.program_id(0)
    tm = x_ref.shape[0]

    @pl.when((i * tm) % seq == 0)
    def _():
        carry_ref[...] = jnp.zeros(carry_ref.shape, F32)

    x = x_ref[...]
    h = _modulated(x, sh_ref[0], sc_ref[0]).astype(BF16)
    y = _dot(h, w_ref[...])
    _put_transposed(vft_ref, 0, y[:, 2 * FOX_W:3 * FOX_W])
    og_ref[...] = y[:, 3 * FOX_W:4 * FOX_W]

    lane = lax.broadcasted_iota(jnp.int32, (tm, LANES), 1)
    z = y[:, 4 * FOX_W:4 * FOX_W + LANES] + bf_ref[...]
    log_f = jnp.where(lane < FOX_HEADS, jnp.minimum(z, 0.0) - jnp.log1p(jnp.exp(-jnp.abs(z))), 0.0)
    tri = tri_ref[...]
    hi, mid, lo = _split3(log_f)
    cum = _dot(tri, hi) + _dot(tri, mid) + _dot(tri, lo) + carry_ref[0:1, :]
    carry_ref[0:1, :] = cum[tm - 1:tm, :]
    cum = cum * LOG2E

    gq, gk = gains_ref[0:1, :], gains_ref[1:2, :]
    g64 = g64_ref[...]
    scale = FOX_HEAD_DIM ** -0.5 * LOG2E

    def head_norm(t, gain):
        ss = _group_sums(t * t, g64)
        return t * lax.rsqrt(ss * (1.0 / FOX_HEAD_DIM) + EPS) * gain

    def bias_lanes(pieces, ones_at, pieces_at, sign):
        blk = jnp.zeros((tm, LANES), F32)
        for ln in ones_at:
            blk = jnp.where(lane == ln, 1.0, blk)
        for ln, p in zip(pieces_at, pieces):
            blk = jnp.where(lane == ln, sign * p.astype(F32), blk)
        return blk

    for s in range(FOX_W // SLAB):
        cols = slice(s * SLAB, (s + 1) * SLAB)
        qn = head_norm(y[:, cols], gq) * scale
        kn = head_norm(y[:, FOX_W + s * SLAB:FOX_W + (s + 1) * SLAB], gk)
        for j in range(SLAB // FOX_HEAD_DIM):
            hd = s * (SLAB // FOX_HEAD_DIM) + j
            blk = slice((j // 2) * LANES, (j // 2 + 1) * LANES)
            qh, kh = qn[:, blk], kn[:, blk]
            if j % 2:
                qh, kh = pltpu.roll(qh, FOX_HEAD_DIM, 1), pltpu.roll(kh, FOX_HEAD_DIM, 1)
            pieces = _split3(jnp.broadcast_to(cum[:, hd:hd + 1], (tm, LANES)))
            q_aug = bias_lanes(pieces, FOX_ONES_Q, FOX_BIAS_Q, 1.0)
            k_aug = bias_lanes(pieces, FOX_BIAS_Q, FOX_ONES_Q, -1.0)
            _put_transposed(qft_ref, hd, jnp.where(lane < FOX_HEAD_DIM, qh, q_aug))
            kf_ref[:, hd * LANES:(hd + 1) * LANES] = jnp.where(lane < FOX_HEAD_DIM, kh, k_aug).astype(BF16)


def _fox_proj(x, mod, w_fox, gains, bf_row, g64, tri, batch, seq):
    t = x.shape[0]
    row = _row_spec(D_MODEL)
    return pl.pallas_call(
        functools.partial(_fox_proj_kernel, seq),
        grid=(t // TILE,),
        in_specs=[row, _mod_spec(seq, 3), _mod_spec(seq, 4), _resident(w_fox.shape),
                  _resident(gains.shape), _resident(bf_row.shape), _resident(g64.shape), _resident(tri.shape)],
        out_specs=[_transposed_out_spec(FOX_HEADS, seq), _row_spec(FOX_HEADS * LANES),
                   _transposed_out_spec(FOX_HEADS // 2, seq), row],
        out_shape=[_transposed_shape(batch, FOX_HEADS, seq), jax.ShapeDtypeStruct((t, FOX_HEADS * LANES), BF16),
                   _transposed_shape(batch, FOX_HEADS // 2, seq), jax.ShapeDtypeStruct((t, FOX_W), F32)],
        scratch_shapes=[pltpu.VMEM((8, LANES), F32)],
        compiler_params=_params(("arbitrary",)),
        name="fox_proj",
    )(x, mod, mod, w_fox, gains, bf_row, g64, tri)


def _tile_order(nq):
    below = [(qi, j) for qi in range(1, nq) for j in range(qi)]
    peeled = len(below) % 2
    unrolled = [(i, i, True) for i in range(nq)] + [(qi, j, False) for qi, j in below[:peeled]]
    looped = below[peeled:]
    table = np.asarray(looped + looped[-1:], np.int32).reshape(-1) if looped else np.zeros((2,), np.int32)
    return unrolled, len(looped), table


def _flash_head(tbl_ref, nq, q_at, k_refs, vt_at, s_buf, m_ref, l_ref, acc_ref):
    n = len(k_refs)
    unrolled, n_looped, _ = _tile_order(nq)

    def issue(qi, j, slot, diagonal):
        off = pl.multiple_of(j * TILE, TILE)
        maxima = []
        for e in range(n):
            s = _dot(k_refs[e][pl.ds(off, TILE), :], q_at[e](qi))
            if diagonal:
                key = lax.broadcasted_iota(jnp.int32, (TILE, TILE), 0)
                query = lax.broadcasted_iota(jnp.int32, (TILE, TILE), 1)
                s = jnp.where(key <= query, s, NEG_INF)
            s_buf[slot, e] = s
            maxima.append(jnp.max(s, axis=0, keepdims=True))
        return tuple(maxima)

    def absorb(qi, j, slot, maxima, first):
        for e in range(n):
            if first:
                m_new = maxima[e]
                p = jnp.exp2(s_buf[slot, e] - m_new)
                l = jnp.sum(p, axis=0, keepdims=True)
                acc = _dot(vt_at[e](j), p.astype(BF16))
            else:
                m = m_ref[e, qi]
                m_new = jnp.maximum(m, maxima[e])
                alpha = jnp.exp2(m - m_new)
                p = jnp.exp2(s_buf[slot, e] - m_new)
                l = alpha * l_ref[e, qi] + jnp.sum(p, axis=0, keepdims=True)
                acc = alpha * acc_ref[e, qi] + _dot(vt_at[e](j), p.astype(BF16))
            m_ref[e, qi] = m_new
            l_ref[e, qi] = l
            acc_ref[e, qi] = acc

    def looped(t):
        return tbl_ref[2 * t], tbl_ref[2 * t + 1]

    pending = issue(*unrolled[0][:2], 0, unrolled[0][2])
    for idx, (qi, j, diagonal) in enumerate(unrolled):
        slot = idx % 2
        ahead = None
        if idx + 1 < len(unrolled):
            nxt = unrolled[idx + 1]
            ahead = issue(nxt[0], nxt[1], 1 - slot, nxt[2])
        elif n_looped:
            ahead = issue(*looped(0), 1 - slot, False)
        absorb(qi, j, slot, pending, diagonal)
        pending = ahead

    if n_looped:
        base = len(unrolled) % 2

        def body(jj, max_a):
            max_b = issue(*looped(2 * jj + 1), 1 - base, False)
            absorb(*looped(2 * jj), base, max_a, False)
            max_a = issue(*looped(2 * jj + 2), base, False)
            absorb(*looped(2 * jj + 1), 1 - base, max_b, False)
            return max_a

        lax.fori_loop(0, n_looped // 2, body, pending)

    return lambda e, qi: acc_ref[e, qi] / l_ref[e, qi]


def _diff_attn_kernel(lambda_init, nq, tbl_ref, qt_ref, k_ref, vt_ref, lam_ref, subln_ref, o_ref, *scratch):
    feat = lax.broadcasted_iota(jnp.int32, (LANES, TILE), 0)
    zero = jnp.zeros((LANES, TILE), BF16)
    q1 = lambda qi: jnp.where(feat < DIFF_HEAD_DIM, qt_ref[0, 0, qi], zero)
    q2 = lambda qi: jnp.where(feat < DIFF_HEAD_DIM, zero, qt_ref[0, 0, qi])
    vt = lambda j: vt_ref[0, 0, j]
    out_t = _flash_head(tbl_ref, nq, [q1, q2], [k_ref, k_ref], [vt, vt], *scratch)
    lv = lam_ref[...]
    lam = (jnp.exp(jnp.sum(lv[0:1] * lv[1:2], axis=-1, keepdims=True))
           - jnp.exp(jnp.sum(lv[2:3] * lv[3:4], axis=-1, keepdims=True)) + lambda_init)
    for qi in range(nq):
        o = (out_t(0, qi) - lam * out_t(1, qi)).T
        o = o * lax.rsqrt(jnp.mean(o * o, axis=-1, keepdims=True) + EPS) * subln_ref[...]
        o_ref[qi * TILE:(qi + 1) * TILE, :] = (o * (1.0 - lambda_init)).astype(BF16)


def _mla_attn_kernel(nq, tbl_ref, qt_ref, k0_ref, k1_ref, vt_ref, o_ref, *scratch):
    out_t = _flash_head(tbl_ref, nq, [lambda qi: qt_ref[0, 0, qi], lambda qi: qt_ref[0, 1, qi]], [k0_ref, k1_ref],
                        [lambda j: vt_ref[0, 0, j], lambda j: vt_ref[0, 1, j]], *scratch)
    for qi in range(nq):
        for e in range(2):
            o_ref[qi * TILE:(qi + 1) * TILE, e * LANES:(e + 1) * LANES] = out_t(e, qi).T.astype(BF16)


def _fox_attn_kernel(nq, tbl_ref, qt_ref, k0_ref, k1_ref, vt_ref, og_ref, o_ref, *scratch):
    vt = lambda j: vt_ref[0, 0, j]
    out_t = _flash_head(tbl_ref, nq, [lambda qi: qt_ref[0, 0, qi], lambda qi: qt_ref[0, 1, qi]], [k0_ref, k1_ref],
                        [vt, vt], *scratch)
    feat = lax.broadcasted_iota(jnp.int32, (LANES, TILE), 0)
    for qi in range(nq):
        rows = slice(qi * TILE, (qi + 1) * TILE)
        o = jnp.where(feat < FOX_HEAD_DIM, out_t(0, qi), out_t(1, qi)).T
        o_ref[rows, :] = (o * _sigmoid(og_ref[rows, :])).astype(BF16)


def _head_tiles_spec(heads, seq):
    return pl.BlockSpec((1, heads, seq // TILE, LANES, TILE), lambda b, h, tbl: (b, h, 0, 0, 0))


def _head_rows_spec(seq, width, head_of=lambda h: h):
    return pl.BlockSpec((seq, width), lambda b, h, tbl: (b, head_of(h)))


def _attn_call(body, name, batch, groups, seq, in_specs, out_spec, out_shape, operands):
    nq = seq // TILE
    table = jnp.asarray(_tile_order(nq)[2])
    stat = pltpu.VMEM((2, nq, 1, TILE), F32)
    return pl.pallas_call(
        functools.partial(body, nq),
        grid_spec=pltpu.PrefetchScalarGridSpec(
            num_scalar_prefetch=1,
            grid=(batch, groups),
            in_specs=in_specs,
            out_specs=out_spec,
            scratch_shapes=[pltpu.VMEM((2, 2, TILE, TILE), F32),
                            stat, stat, pltpu.VMEM((2, nq, LANES, TILE), F32)],
        ),
        out_shape=out_shape,
        compiler_params=_params(("parallel", "parallel")),
        name=name,
    )(table, *operands)


def _diff_attn(qat, ka, vat, lam_rows, subln, lambda_init, batch, seq):
    const = lambda shape: pl.BlockSpec(shape, lambda b, h, tbl: (0, 0))
    return _attn_call(
        functools.partial(_diff_attn_kernel, lambda_init), "diff_attn", batch, DIFF_HEADS, seq,
        [_head_tiles_spec(1, seq), _head_rows_spec(seq, LANES), _head_tiles_spec(1, seq),
         const(lam_rows.shape), const(subln.shape)],
        _head_rows_spec(seq, LANES), jax.ShapeDtypeStruct(ka.shape, BF16), (qat, ka, vat, lam_rows, subln))


def _mla_attn(qbt, kb, vbt, batch, seq):
    return _attn_call(
        _mla_attn_kernel, "mla_attn", batch, MLA_HEADS // 2, seq,
        [_head_tiles_spec(2, seq), _head_rows_spec(seq, LANES, lambda p: 2 * p),
         _head_rows_spec(seq, LANES, lambda p: 2 * p + 1), _head_tiles_spec(2, seq)],
        _head_rows_spec(seq, 2 * LANES), jax.ShapeDtypeStruct(kb.shape, BF16), (qbt, kb, kb, vbt))


def _fox_attn(qft, kf, vft, og, batch, seq):
    return _attn_call(
        _fox_attn_kernel, "fox_attn", batch, FOX_HEADS // 2, seq,
        [_head_tiles_spec(2, seq), _head_rows_spec(seq, LANES, lambda p: 2 * p),
         _head_rows_spec(seq, LANES, lambda p: 2 * p + 1), _head_tiles_spec(1, seq), _head_rows_spec(seq, LANES)],
        _head_rows_spec(seq, LANES), jax.ShapeDtypeStruct(og.shape, BF16), (qft, kf, kf, vft, og))


def _out_proj_kernel(n_in, *refs):
    x_ref, g_ref = refs[0], refs[1]
    o_refs = refs[2:2 + n_in]
    w_refs = refs[2 + n_in:2 + 2 * n_in]
    out_ref = refs[2 + 2 * n_in]
    mix = _dot(o_refs[0][...], w_refs[0][...])
    for o_ref, w_ref in zip(o_refs[1:], w_refs[1:]):
        mix = mix + _dot(o_ref[...], w_ref[...])
    out_ref[...] = x_ref[...] + g_ref[0] * mix


def _out_proj(x, mod, os, ws, seq):
    t = x.shape[0]
    row = _row_spec(D_MODEL)
    return pl.pallas_call(
        functools.partial(_out_proj_kernel, len(os)),
        grid=(t // TILE,),
        in_specs=[row, _mod_spec(seq, 5)] + [_row_spec(o.shape[1]) for o in os] + [_resident(w.shape) for w in ws],
        out_specs=row,
        out_shape=jax.ShapeDtypeStruct(x.shape, F32),
        compiler_params=_params(("parallel",)),
        name="out_proj",
    )(x, mod, *os, *ws)


def _lane_rows(rows):
    width = max(r.shape[0] for r in rows)
    out = jnp.zeros((8, width), F32)
    for i, r in enumerate(rows):
        out = out.at[i, :r.shape[0]].set(r.astype(F32))
    return out


def _block_diag_ones(n, group):
    idx = np.arange(n) // group
    return jnp.asarray((idx[:, None] == idx[None, :]).astype(np.float32), BF16)


def _pad_heads(w, heads, dim):
    r = w.shape[0]
    w = w.reshape(r, heads, dim)
    return jnp.pad(w, ((0, 0), (0, 0), (0, LANES - dim))).reshape(r, heads * LANES)


def kernel(x, c, positions, ada_w, ada_b, ff1_gate, ff1_up, ff1_down, ff2_gate, ff2_up, ff2_down, ab_w_in, mla_w_qb, mla_w_kvb, mla_q_lat_g, mla_kv_lat_g, diff_q_g, diff_k_g, mla_q_g, mla_k_g, diff_lam_q1, diff_lam_k1, diff_lam_q2, diff_lam_k2, diff_subln_g, ab_w_out, fox_w_in, fox_b_f, fox_q_g, fox_k_g, fox_w_out):
    batch, seq, d = x.shape
    depth = ada_w.shape[0]
    t = batch * seq
    assert d == D_MODEL and seq % TILE == 0 and batch <= 8

    c_pad = jnp.zeros((8, d), F32).at[:batch].set(c)
    mod_all = _adaln(c_pad, ada_w, ada_b)

    inv_a = ROPE_THETA ** (-jnp.arange(0, DIFF_HEAD_DIM, 2, dtype=F32) / DIFF_HEAD_DIM)
    inv_b = ROPE_THETA ** (-jnp.arange(0, MLA_ROPE_DIM, 2, dtype=F32) / MLA_ROPE_DIM)
    inv_rows = jnp.zeros((8, LANES), F32).at[0, :32].set(inv_a).at[0, 64:80].set(inv_b)
    lane = np.arange(LANES)
    sgn_a = np.where(lane % 64 < 32, -1.0, 1.0)
    sgn_b = np.where((lane >= 64) & (lane < 80), -1.0, np.where((lane >= 80) & (lane < 96), 1.0, 0.0))
    sgn_rows = jnp.asarray(np.stack([sgn_a, sgn_b] + [np.zeros(LANES)] * 6), F32)
    pos_b = jnp.broadcast_to(positions.astype(F32).reshape(t, 1), (t, LANES))
    tables = _rope_tables(pos_b, inv_rows, sgn_rows)

    g64 = _block_diag_ones(SLAB, 64)
    g128 = _block_diag_ones(SLAB, LANES)
    bf = lambda w: w.astype(BF16)
    ff1 = [_to_bf16(w) for w in (ff1_gate, ff1_up, ff1_down)]
    ff2 = [_to_bf16(w) for w in (ff2_gate, ff2_up, ff2_down)]

    xt = x.reshape(t, d)
    for l in range(depth):
        mod = mod_all[l, :batch].reshape(batch * N_MOD, 1, d)
        xt = _ffn(xt, mod, 0, l, *ff1, seq)
        if l % 2 == 0:
            e = l // 2
            lambda_init = 0.8 - 0.6 * math.exp(-0.3 * l)
            w_in = ab_w_in[e]
            o_cq = 3 * DIFF_W
            o_kr = o_cq + MLA_Q_RANK + MLA_KV_RANK
            kr_pad = jnp.zeros((d, LANES), F32).at[:, MLA_NOPE_DIM:MLA_QK_DIM].set(w_in[:, o_kr:])
            w_all = bf(jnp.concatenate([w_in[:, :o_kr], kr_pad], axis=1))
            wqb = bf(_pad_heads(mla_w_qb[e], MLA_HEADS, MLA_QK_DIM))
            wkv = mla_w_kvb[e].reshape(MLA_KV_RANK, MLA_HEADS, MLA_NOPE_DIM + MLA_V_DIM)
            wkn = bf(_pad_heads(wkv[:, :, :MLA_NOPE_DIM].reshape(MLA_KV_RANK, -1), MLA_HEADS, MLA_NOPE_DIM))
            wvb = bf(wkv[:, :, MLA_NOPE_DIM:].reshape(MLA_KV_RANK, -1))
            pad128 = lambda g: jnp.pad(g, (0, LANES - g.shape[0]))
            gains = _lane_rows([jnp.tile(diff_q_g[e], 4), jnp.tile(diff_k_g[e], 4),
                                jnp.tile(pad128(mla_q_g[e]), 2), jnp.tile(pad128(mla_k_g[e]), 2)])
            qat, ka, vat, qbt, kb, vbt = _ab_proj(
                xt, mod, w_all, wqb, wkn, wvb, gains, mla_q_lat_g[e].reshape(1, -1), mla_kv_lat_g[e].reshape(1, -1),
                g64, g128, tables, batch, seq)
            lam_rows = _lane_rows([pad128(diff_lam_q1[e]), pad128(diff_lam_k1[e]),
                                   pad128(diff_lam_q2[e]), pad128(diff_lam_k2[e])])
            o_a = _diff_attn(qat, ka, vat, lam_rows, diff_subln_g[e].reshape(1, -1), lambda_init, batch, seq)
            o_b = _mla_attn(qbt, kb, vbt, batch, seq)
            w_out = bf(ab_w_out[e])
            xt = _out_proj(xt, mod, [o_a, o_b], [w_out[:DIFF_W], w_out[DIFF_W:]], seq)
        else:
            o = l // 2
            w_fox = bf(jnp.pad(fox_w_in[o], ((0, 0), (0, LANES - FOX_HEADS))))
            gains = _lane_rows([jnp.tile(fox_q_g[o], 4), jnp.tile(fox_k_g[o], 4)])
            bf_row = jnp.pad(fox_b_f[o], (0, LANES - FOX_HEADS)).reshape(1, LANES)
            tri = jnp.asarray(np.tril(np.ones((TILE, TILE), np.float32)), BF16)
            qft, kf, vft, og = _fox_proj(xt, mod, w_fox, gains, bf_row, g64, tri, batch, seq)
            o_f = _fox_attn(qft, kf, vft, og, batch, seq)
            xt = _out_proj(xt, mod, [o_f], [bf(fox_w_out[o])], seq)
        xt = _ffn(xt, mod, 6, l, *ff2, seq)
    return xt.reshape(batch, seq, d)
```

```python
import functools
import math

import numpy as np
import jax
import jax.numpy as jnp
from jax import lax
from jax.experimental import pallas as pl
from jax.experimental.pallas import tpu as pltpu

F32 = jnp.float32
BF16 = jnp.bfloat16

D_MODEL = 1024
D_FF = 2816
N_MOD = 9
ROPE_THETA = 10000.0
EPS = 1e-6
NEG_INF = -1e30
LOG2E = 1.4426950408889634

DIFF_HEADS = 8
DIFF_HEAD_DIM = 64
MLA_HEADS = 8
MLA_NOPE_DIM = 64
MLA_ROPE_DIM = 32
MLA_V_DIM = 128
MLA_Q_RANK = 384
MLA_KV_RANK = 256
FOX_HEADS = 16
FOX_HEAD_DIM = 64

DIFF_W = DIFF_HEADS * 2 * DIFF_HEAD_DIM
MLA_QK_DIM = MLA_NOPE_DIM + MLA_ROPE_DIM
FOX_W = FOX_HEADS * FOX_HEAD_DIM

LANES = 128
SLAB = 256
VMEM_LIMIT = 56 * 1024 * 1024

TILE = 512
FF_CHUNK = 256

FOX_ONES_Q = (64, 65, 66)
FOX_BIAS_Q = (67, 68, 69)


def _params(sem):
    return pltpu.CompilerParams(dimension_semantics=sem, vmem_limit_bytes=VMEM_LIMIT)


def _dot(a, b):
    return jnp.dot(a, b, preferred_element_type=F32)


def _sigmoid(x):
    return 1.0 / (1.0 + jnp.exp(-x))


def _modulated(x, shift, scale):
    ms = jnp.mean(x * x, axis=-1, keepdims=True)
    return x * lax.rsqrt(ms + EPS) * (1.0 + scale) + shift


def _split3(x):
    hi = x.astype(BF16)
    r = x - hi.astype(F32)
    mid = r.astype(BF16)
    lo = (r - mid.astype(F32)).astype(BF16)
    return hi, mid, lo


def _group_sums(sq, gmat):
    return _dot(sq.astype(BF16), gmat)


def _rope(t, cos, sin_signed, half):
    n = t.shape[-1]
    lane = lax.broadcasted_iota(jnp.int32, t.shape, 1)
    up = pltpu.roll(t, n - half, 1)
    dn = pltpu.roll(t, half, 1)
    partner = jnp.where((lane & half) == 0, up, dn)
    return t * cos + partner * sin_signed


def _run_staged(stages):
    pending = stages[0][0]()
    for i, (_, consume) in enumerate(stages):
        ahead = stages[i + 1][0]() if i + 1 < len(stages) else None
        consume(pending)
        pending = ahead


def _put_transposed(ref, first_head, t):
    for j in range(t.shape[1] // LANES):
        ref[0, first_head + j, 0] = t[:, j * LANES:(j + 1) * LANES].T.astype(BF16)


def _cast_kernel(w_ref, o_ref):
    o_ref[...] = w_ref[...].astype(BF16)


def _to_bf16(w):
    depth, rows, cols = w.shape
    rb = 256
    spec = pl.BlockSpec((1, rb, cols), lambda l, i: (l, i, 0))
    return pl.pallas_call(
        _cast_kernel,
        grid=(depth, rows // rb),
        in_specs=[spec],
        out_specs=spec,
        out_shape=jax.ShapeDtypeStruct(w.shape, BF16),
        compiler_params=_params(("parallel", "parallel")),
        name="to_bf16",
    )(w)


def _adaln_kernel(c_ref, w_ref, b_ref, o_ref):
    c = c_ref[...]
    cond = (c * _sigmoid(c)).astype(BF16)
    o_ref[0] = _dot(cond, w_ref[0].astype(BF16)) + b_ref[0]


def _adaln(c_pad, ada_w, ada_b):
    depth, d, n = ada_w.shape
    tn = 1024
    return pl.pallas_call(
        _adaln_kernel,
        grid=(depth, n // tn),
        in_specs=[
            pl.BlockSpec((8, d), lambda l, j: (0, 0)),
            pl.BlockSpec((1, d, tn), lambda l, j: (l, 0, j)),
            pl.BlockSpec((1, 1, tn), lambda l, j: (l, 0, j)),
        ],
        out_specs=pl.BlockSpec((1, 8, tn), lambda l, j: (l, 0, j)),
        out_shape=jax.ShapeDtypeStruct((depth, 8, n), F32),
        compiler_params=_params(("parallel", "parallel")),
        name="adaln",
    )(c_pad, ada_w, ada_b.reshape(depth, 1, n))


def _rope_table_kernel(pos_ref, inv_ref, sgn_ref, cos_a_ref, sin_a_ref, cos_b_ref, sin_b_ref):
    ang = pos_ref[...] * inv_ref[0:1, :]
    c = jnp.cos(ang)
    s = jnp.sin(ang)
    lane = lax.broadcasted_iota(jnp.int32, ang.shape, 1)
    in_a = lane < 32
    in_b = (lane >= 64) & (lane < 80)
    rope_b = (lane >= 64) & (lane < 96)

    def spread_a(t):
        t = jnp.where(in_a, t, 0.0)
        t = t + pltpu.roll(t, 32, 1)
        return t + pltpu.roll(t, 64, 1)

    def spread_b(t):
        t = jnp.where(in_b, t, 0.0)
        return t + pltpu.roll(t, 16, 1)

    cos_a_ref[...] = spread_a(c)
    sin_a_ref[...] = spread_a(s) * sgn_ref[0:1, :]
    cos_b_ref[...] = jnp.where(rope_b, spread_b(c), 1.0)
    sin_b_ref[...] = spread_b(s) * sgn_ref[1:2, :]


def _rope_tables(pos_b, inv_rows, sgn_rows):
    t = pos_b.shape[0]
    ts = TILE
    spec = pl.BlockSpec((ts, LANES), lambda i: (i, 0))
    cspec = pl.BlockSpec((8, LANES), lambda i: (0, 0))
    out = jax.ShapeDtypeStruct((t, LANES), F32)
    return pl.pallas_call(
        _rope_table_kernel,
        grid=(t // ts,),
        in_specs=[spec, cspec, cspec],
        out_specs=[spec, spec, spec, spec],
        out_shape=[out, out, out, out],
        compiler_params=_params(("parallel",)),
        name="rope_tables",
    )(pos_b, inv_rows, sgn_rows)


def _ffn_kernel(x_ref, sh_ref, sc_ref, g_ref, wg_ref, wu_ref, wd_ref, o_ref):
    x = x_ref[...]
    h = _modulated(x, sh_ref[0], sc_ref[0]).astype(BF16)
    acc = jnp.zeros(x.shape, F32)
    for c in range(D_FF // FF_CHUNK):
        cols = slice(c * FF_CHUNK, (c + 1) * FF_CHUNK)
        g = _dot(h, wg_ref[0, :, cols])
        u = _dot(h, wu_ref[0, :, cols])
        a = (g * _sigmoid(g) * u).astype(BF16)
        acc = acc + _dot(a, wd_ref[0, cols, :])
    o_ref[...] = x + (0.5 * g_ref[0]) * acc


def _mod_spec(seq, which):
    return pl.BlockSpec((1, 1, D_MODEL), lambda i: ((i * TILE // seq) * N_MOD + which, 0, 0))


def _resident(shape):
    return pl.BlockSpec(shape, lambda i: (0,) * len(shape), pipeline_mode=pl.Buffered(1))


def _row_spec(width):
    return pl.BlockSpec((TILE, width), lambda i: (i, 0))


def _ffn(x, mod, which, layer, wg, wu, wd, seq):
    t = x.shape[0]
    row = _row_spec(D_MODEL)
    of_layer = lambda w: pl.BlockSpec((1,) + w.shape[1:], lambda i: (layer, 0, 0), pipeline_mode=pl.Buffered(1))
    return pl.pallas_call(
        _ffn_kernel,
        grid=(t // TILE,),
        in_specs=[row, _mod_spec(seq, which), _mod_spec(seq, which + 1), _mod_spec(seq, which + 2),
                  of_layer(wg), of_layer(wu), of_layer(wd)],
        out_specs=row,
        out_shape=jax.ShapeDtypeStruct(x.shape, F32),
        compiler_params=_params(("parallel",)),
        name="ffn",
    )(x, mod, mod, mod, wg, wu, wd)


def _transposed_shape(batch, heads, seq):
    return jax.ShapeDtypeStruct((batch, heads, seq // TILE, LANES, TILE), BF16)


def _transposed_out_spec(heads, seq):
    chunks = seq // TILE
    return pl.BlockSpec((1, heads, 1, LANES, TILE), lambda i: (i // chunks, 0, i % chunks, 0, 0))


def _ab_proj_kernel(x_ref, sh_ref, sc_ref, w_ref, wqb_ref, wkn_ref, wvb_ref, gains_ref, lat_q_ref, lat_kv_ref,
                    g64_ref, g128_ref, cos_a_ref, sin_a_ref, cos_b_ref, sin_b_ref,
                    qat_ref, ka_ref, vat_ref, qbt_ref, kb_ref, vbt_ref):
    x = x_ref[...]
    h = _modulated(x, sh_ref[0], sc_ref[0]).astype(BF16)
    two = lambda r: jnp.concatenate([r, r], axis=1)
    cos_a, sin_a = two(cos_a_ref[...]), two(sin_a_ref[...])
    cos_b, sin_b = two(cos_b_ref[...]), two(sin_b_ref[...])
    gq_a, gk_a = gains_ref[0:1, :], gains_ref[1:2, :]
    gq_b, gk_b = gains_ref[2:3, :], gains_ref[3:4, :]
    g64, g128 = g64_ref[...], g128_ref[...]
    scale_a = DIFF_HEAD_DIM ** -0.5 * LOG2E
    scale_b = MLA_QK_DIM ** -0.5 * LOG2E
    heads_per_slab = SLAB // LANES
    o_cq = 3 * DIFF_W
    o_ckv = o_cq + MLA_Q_RANK
    o_kr = o_ckv + MLA_KV_RANK
    latent = {}

    def head_norm(t, gmat, dim, gain):
        ss = _group_sums(t * t, gmat)
        return t * lax.rsqrt(ss * (1.0 / dim) + EPS) * gain

    def project(cols):
        return lambda: _dot(h, w_ref[:, cols])

    def latent_inputs(y):
        c_q = y[:, :MLA_Q_RANK]
        c_q = c_q * lax.rsqrt(jnp.mean(c_q * c_q, axis=-1, keepdims=True) + EPS) * lat_q_ref[...]
        c_kv = y[:, MLA_Q_RANK:MLA_Q_RANK + MLA_KV_RANK]
        c_kv = c_kv * lax.rsqrt(jnp.mean(c_kv * c_kv, axis=-1, keepdims=True) + EPS) * lat_kv_ref[...]
        latent["q"], latent["kv"] = c_q.astype(BF16), c_kv.astype(BF16)
        latent["k_rope"] = two(y[:, MLA_Q_RANK + MLA_KV_RANK:])

    def diff_q(s):
        def consume(y):
            q = _rope(head_norm(y, g64, DIFF_HEAD_DIM, gq_a), cos_a, sin_a, 32)
            _put_transposed(qat_ref, heads_per_slab * s, q * scale_a)
        return consume

    def diff_k(s):
        def consume(y):
            k = _rope(head_norm(y, g64, DIFF_HEAD_DIM, gk_a), cos_a, sin_a, 32)
            ka_ref[:, s * SLAB:(s + 1) * SLAB] = k.astype(BF16)
        return consume

    def mla_q(s):
        def consume(y):
            q = _rope(head_norm(y, g128, MLA_QK_DIM, gq_b), cos_b, sin_b, 16)
            _put_transposed(qbt_ref, heads_per_slab * s, q * scale_b)
        return consume

    def mla_k(s):
        def consume(y):
            k = _rope(head_norm(y + latent["k_rope"], g128, MLA_QK_DIM, gk_b), cos_b, sin_b, 16)
            kb_ref[:, s * SLAB:(s + 1) * SLAB] = k.astype(BF16)
        return consume

    slab = lambda base, s: slice(base + s * SLAB, base + (s + 1) * SLAB)
    n_slabs = DIFF_W // SLAB
    stages = [(project(slice(o_cq, o_kr + LANES)), latent_inputs)]
    for s in range(n_slabs):
        stages.append((project(slab(0, s)), diff_q(s)))
        stages.append((project(slab(DIFF_W, s)), diff_k(s)))
        stages.append((project(slab(2 * DIFF_W, s)),
                       lambda y, s=s: _put_transposed(vat_ref, heads_per_slab * s, y)))
    for s in range(n_slabs):
        cols = slab(0, s)
        stages.append((lambda cols=cols: _dot(latent["q"], wqb_ref[:, cols]), mla_q(s)))
        stages.append((lambda cols=cols: _dot(latent["kv"], wkn_ref[:, cols]), mla_k(s)))
        stages.append((lambda cols=cols: _dot(latent["kv"], wvb_ref[:, cols]),
                       lambda y, s=s: _put_transposed(vbt_ref, heads_per_slab * s, y)))
    _run_staged(stages)


def _ab_proj(x, mod, w_all, wqb, wkn, wvb, gains, lat_q, lat_kv, g64, g128, tables, batch, seq):
    t = x.shape[0]
    row = _row_spec(D_MODEL)
    tab = _row_spec(LANES)
    rows_out = jax.ShapeDtypeStruct((t, D_MODEL), BF16)
    tr_out = _transposed_shape(batch, DIFF_HEADS, seq)
    tr_spec = _transposed_out_spec(DIFF_HEADS, seq)
    return pl.pallas_call(
        _ab_proj_kernel,
        grid=(t // TILE,),
        in_specs=[row, _mod_spec(seq, 3), _mod_spec(seq, 4),
                  _resident(w_all.shape), _resident(wqb.shape), _resident(wkn.shape), _resident(wvb.shape),
                  _resident(gains.shape), _resident(lat_q.shape), _resident(lat_kv.shape),
                  _resident(g64.shape), _resident(g128.shape), tab, tab, tab, tab],
        out_specs=[tr_spec, row, tr_spec, tr_spec, row, tr_spec],
        out_shape=[tr_out, rows_out, tr_out, tr_out, rows_out, tr_out],
        compiler_params=_params(("parallel",)),
        name="ab_proj",
    )(x, mod, mod, w_all, wqb, wkn, wvb, gains, lat_q, lat_kv, g64, g128, *tables)


def _fox_proj_kernel(seq, x_ref, sh_ref, sc_ref, w_ref, gains_ref, bf_ref, g64_ref, tri_ref,
                     qft_ref, kf_ref, vft_ref, og_ref, carry_ref):
    i = pl.program_id(0)
    tm = x_ref.shape[0]

    @pl.when((i * tm) % seq == 0)
    def _():
        carry_ref[...] = jnp.zeros(carry_ref.shape, F32)

    x = x_ref[...]
    h = _modulated(x, sh_ref[0], sc_ref[0]).astype(BF16)
    lane = lax.broadcasted_iota(jnp.int32, (tm, LANES), 1)
    gq, gk = gains_ref[0:1, :], gains_ref[1:2, :]
    g64 = g64_ref[...]
    scale = FOX_HEAD_DIM ** -0.5 * LOG2E
    heads_per_slab = SLAB // FOX_HEAD_DIM
    gate = {}

    def project(cols):
        return lambda: _dot(h, w_ref[:, cols])

    def forget_cumsum(f):
        z = f + bf_ref[...]
        log_f = jnp.where(lane < FOX_HEADS, jnp.minimum(z, 0.0) - jnp.log1p(jnp.exp(-jnp.abs(z))), 0.0)
        tri = tri_ref[...]
        hi, mid, lo = _split3(log_f)
        cum = _dot(tri, hi) + _dot(tri, mid) + _dot(tri, lo) + carry_ref[0:1, :]
        carry_ref[0:1, :] = cum[tm - 1:tm, :]
        gate["cum"] = cum * LOG2E

    def head_norm(t, gain):
        ss = _group_sums(t * t, g64)
        return t * lax.rsqrt(ss * (1.0 / FOX_HEAD_DIM) + EPS) * gain

    def bias_lanes(pieces, ones_at, pieces_at, sign):
        blk = jnp.zeros((tm, LANES), F32)
        for ln in ones_at:
            blk = jnp.where(lane == ln, 1.0, blk)
        for ln, p in zip(pieces_at, pieces):
            blk = jnp.where(lane == ln, sign * p.astype(F32), blk)
        return blk

    def head_blocks(t, s, is_q):
        for j in range(heads_per_slab):
            hd = s * heads_per_slab + j
            th = t[:, (j // 2) * LANES:(j // 2 + 1) * LANES]
            if j % 2:
                th = pltpu.roll(th, FOX_HEAD_DIM, 1)
            pieces = _split3(jnp.broadcast_to(gate["cum"][:, hd:hd + 1], (tm, LANES)))
            if is_q:
                aug = bias_lanes(pieces, FOX_ONES_Q, FOX_BIAS_Q, 1.0)
                _put_transposed(qft_ref, hd, jnp.where(lane < FOX_HEAD_DIM, th, aug))
            else:
                aug = bias_lanes(pieces, FOX_BIAS_Q, FOX_ONES_Q, -1.0)
                kf_ref[:, hd * LANES:(hd + 1) * LANES] = jnp.where(lane < FOX_HEAD_DIM, th, aug).astype(BF16)

    def store_gate(s):
        def consume(y):
            og_ref[:, s * SLAB:(s + 1) * SLAB] = y
        return consume

    slab = lambda base, s: slice(base + s * SLAB, base + (s + 1) * SLAB)
    stages = [(project(slice(4 * FOX_W, 4 * FOX_W + LANES)), forget_cumsum)]
    for s in range(FOX_W // SLAB):
        stages.append((project(slab(2 * FOX_W, s)),
                       lambda y, s=s: _put_transposed(vft_ref, (SLAB // LANES) * s, y)))
        stages.append((project(slab(3 * FOX_W, s)), store_gate(s)))
    for s in range(FOX_W // SLAB):
        stages.append((project(slab(0, s)), lambda y, s=s: head_blocks(head_norm(y, gq) * scale, s, True)))
        stages.append((project(slab(FOX_W, s)), lambda y, s=s: head_blocks(head_norm(y, gk), s, False)))
    _run_staged(stages)


def _fox_proj(x, mod, w_fox, gains, bf_row, g64, tri, batch, seq):
    t = x.shape[0]
    row = _row_spec(D_MODEL)
    return pl.pallas_call(
        functools.partial(_fox_proj_kernel, seq),
        grid=(t // TILE,),
        in_specs=[row, _mod_spec(seq, 3), _mod_spec(seq, 4), _resident(w_fox.shape),
                  _resident(gains.shape), _resident(bf_row.shape), _resident(g64.shape), _resident(tri.shape)],
        out_specs=[_transposed_out_spec(FOX_HEADS, seq), _row_spec(FOX_HEADS * LANES),
                   _transposed_out_spec(FOX_HEADS // 2, seq), row],
        out_shape=[_transposed_shape(batch, FOX_HEADS, seq), jax.ShapeDtypeStruct((t, FOX_HEADS * LANES), BF16),
                   _transposed_shape(batch, FOX_HEADS // 2, seq), jax.ShapeDtypeStruct((t, FOX_W), F32)],
        scratch_shapes=[pltpu.VMEM((8, LANES), F32)],
        compiler_params=_params(("arbitrary",)),
        name="fox_proj",
    )(x, mod, mod, w_fox, gains, bf_row, g64, tri)


def _tile_order(nq):
    below = [(qi, j) for qi in range(1, nq) for j in range(qi)]
    peeled = len(below) % 2
    unrolled = [(i, i, True) for i in range(nq)] + [(qi, j, False) for qi, j in below[:peeled]]
    looped = below[peeled:]
    table = np.asarray(looped + looped[-1:], np.int32).reshape(-1) if looped else np.zeros((2,), np.int32)
    return unrolled, len(looped), table


def _flash_head(tbl_ref, nq, q_at, k_refs, vt_at, s_buf, m_ref, l_ref, acc_ref):
    n = len(k_refs)
    unrolled, n_looped, _ = _tile_order(nq)

    def issue(qi, j, slot, diagonal):
        off = pl.multiple_of(j * TILE, TILE)
        maxima = []
        for e in range(n):
            s = _dot(k_refs[e][pl.ds(off, TILE), :], q_at[e](qi))
            if diagonal:
                key = lax.broadcasted_iota(jnp.int32, (TILE, TILE), 0)
                query = lax.broadcasted_iota(jnp.int32, (TILE, TILE), 1)
                s = jnp.where(key <= query, s, NEG_INF)
            s_buf[slot, e] = s
            maxima.append(jnp.max(s, axis=0, keepdims=True))
        return tuple(maxima)

    def absorb(qi, j, slot, maxima, first):
        for e in range(n):
            if first:
                m_new = maxima[e]
                p = jnp.exp2(s_buf[slot, e] - m_new)
                l = jnp.sum(p, axis=0, keepdims=True)
                acc = _dot(vt_at[e](j), p.astype(BF16))
            else:
                m = m_ref[e, qi]
                m_new = jnp.maximum(m, maxima[e])
                alpha = jnp.exp2(m - m_new)
                p = jnp.exp2(s_buf[slot, e] - m_new)
                l = alpha * l_ref[e, qi] + jnp.sum(p, axis=0, keepdims=True)
                acc = alpha * acc_ref[e, qi] + _dot(vt_at[e](j), p.astype(BF16))
            m_ref[e, qi] = m_new
            l_ref[e, qi] = l
            acc_ref[e, qi] = acc

    def looped(t):
        return tbl_ref[2 * t], tbl_ref[2 * t + 1]

    pending = issue(*unrolled[0][:2], 0, unrolled[0][2])
    for idx, (qi, j, diagonal) in enumerate(unrolled):
        slot = idx % 2
        ahead = None
        if idx + 1 < len(unrolled):
            nxt = unrolled[idx + 1]
            ahead = issue(nxt[0], nxt[1], 1 - slot, nxt[2])
        elif n_looped:
            ahead = issue(*looped(0), 1 - slot, False)
        absorb(qi, j, slot, pending, diagonal)
        pending = ahead

    if n_looped:
        base = len(unrolled) % 2

        def body(jj, max_a):
            max_b = issue(*looped(2 * jj + 1), 1 - base, False)
            absorb(*looped(2 * jj), base, max_a, False)
            max_a = issue(*looped(2 * jj + 2), base, False)
            absorb(*looped(2 * jj + 1), 1 - base, max_b, False)
            return max_a

        lax.fori_loop(0, n_looped // 2, body, pending)

    return lambda e, qi: acc_ref[e, qi] / l_ref[e, qi]


def _diff_attn_kernel(lambda_init, nq, tbl_ref, qt_ref, k_ref, vt_ref, lam_ref, subln_ref, o_ref, *scratch):
    feat = lax.broadcasted_iota(jnp.int32, (LANES, TILE), 0)
    zero = jnp.zeros((LANES, TILE), BF16)
    q1 = lambda qi: jnp.where(feat < DIFF_HEAD_DIM, qt_ref[0, 0, qi], zero)
    q2 = lambda qi: jnp.where(feat < DIFF_HEAD_DIM, zero, qt_ref[0, 0, qi])
    vt = lambda j: vt_ref[0, 0, j]
    out_t = _flash_head(tbl_ref, nq, [q1, q2], [k_ref, k_ref], [vt, vt], *scratch)
    lv = lam_ref[...]
    lam = (jnp.exp(jnp.sum(lv[0:1] * lv[1:2], axis=-1, keepdims=True))
           - jnp.exp(jnp.sum(lv[2:3] * lv[3:4], axis=-1, keepdims=True)) + lambda_init)
    for qi in range(nq):
        o = (out_t(0, qi) - lam * out_t(1, qi)).T
        o = o * lax.rsqrt(jnp.mean(o * o, axis=-1, keepdims=True) + EPS) * subln_ref[...]
        o_ref[qi * TILE:(qi + 1) * TILE, :] = (o * (1.0 - lambda_init)).astype(BF16)


def _mla_attn_kernel(nq, tbl_ref, qt_ref, k0_ref, k1_ref, vt_ref, o_ref, *scratch):
    out_t = _flash_head(tbl_ref, nq, [lambda qi: qt_ref[0, 0, qi], lambda qi: qt_ref[0, 1, qi]], [k0_ref, k1_ref],
                        [lambda j: vt_ref[0, 0, j], lambda j: vt_ref[0, 1, j]], *scratch)
    for qi in range(nq):
        for e in range(2):
            o_ref[qi * TILE:(qi + 1) * TILE, e * LANES:(e + 1) * LANES] = out_t(e, qi).T.astype(BF16)


def _fox_attn_kernel(nq, tbl_ref, qt_ref, k0_ref, k1_ref, vt_ref, og_ref, o_ref, *scratch):
    vt = lambda j: vt_ref[0, 0, j]
    out_t = _flash_head(tbl_ref, nq, [lambda qi: qt_ref[0, 0, qi], lambda qi: qt_ref[0, 1, qi]], [k0_ref, k1_ref],
                        [vt, vt], *scratch)
    feat = lax.broadcasted_iota(jnp.int32, (LANES, TILE), 0)
    for qi in range(nq):
        rows = slice(qi * TILE, (qi + 1) * TILE)
        o = jnp.where(feat < FOX_HEAD_DIM, out_t(0, qi), out_t(1, qi)).T
        o_ref[rows, :] = (o * _sigmoid(og_ref[rows, :])).astype(BF16)


def _head_tiles_spec(heads, seq):
    return pl.BlockSpec((1, heads, seq // TILE, LANES, TILE), lambda b, h, tbl: (b, h, 0, 0, 0))


def _head_rows_spec(seq, width, head_of=lambda h: h):
    return pl.BlockSpec((seq, width), lambda b, h, tbl: (b, head_of(h)))


def _attn_call(body, name, batch, groups, seq, in_specs, out_spec, out_shape, operands):
    nq = seq // TILE
    table = jnp.asarray(_tile_order(nq)[2])
    stat = pltpu.VMEM((2, nq, 1, TILE), F32)
    return pl.pallas_call(
        functools.partial(body, nq),
        grid_spec=pltpu.PrefetchScalarGridSpec(
            num_scalar_prefetch=1,
            grid=(batch, groups),
            in_specs=in_specs,
            out_specs=out_spec,
            scratch_shapes=[pltpu.VMEM((2, 2, TILE, TILE), F32),
                            stat, stat, pltpu.VMEM((2, nq, LANES, TILE), F32)],
        ),
        out_shape=out_shape,
        compiler_params=_params(("parallel", "parallel")),
        name=name,
    )(table, *operands)


def _diff_attn(qat, ka, vat, lam_rows, subln, lambda_init, batch, seq):
    const = lambda shape: pl.BlockSpec(shape, lambda b, h, tbl: (0, 0))
    return _attn_call(
        functools.partial(_diff_attn_kernel, lambda_init), "diff_attn", batch, DIFF_HEADS, seq,
        [_head_tiles_spec(1, seq), _head_rows_spec(seq, LANES), _head_tiles_spec(1, seq),
         const(lam_rows.shape), const(subln.shape)],
        _head_rows_spec(seq, LANES), jax.ShapeDtypeStruct(ka.shape, BF16), (qat, ka, vat, lam_rows, subln))


def _mla_attn(qbt, kb, vbt, batch, seq):
    return _attn_call(
        _mla_attn_kernel, "mla_attn", batch, MLA_HEADS // 2, seq,
        [_head_tiles_spec(2, seq), _head_rows_spec(seq, LANES, lambda p: 2 * p),
         _head_rows_spec(seq, LANES, lambda p: 2 * p + 1), _head_tiles_spec(2, seq)],
        _head_rows_spec(seq, 2 * LANES), jax.ShapeDtypeStruct(kb.shape, BF16), (qbt, kb, kb, vbt))


def _fox_attn(qft, kf, vft, og, batch, seq):
    return _attn_call(
        _fox_attn_kernel, "fox_attn", batch, FOX_HEADS // 2, seq,
        [_head_tiles_spec(2, seq), _head_rows_spec(seq, LANES, lambda p: 2 * p),
         _head_rows_spec(seq, LANES, lambda p: 2 * p + 1), _head_tiles_spec(1, seq), _head_rows_spec(seq, LANES)],
        _head_rows_spec(seq, LANES), jax.ShapeDtypeStruct(og.shape, BF16), (qft, kf, kf, vft, og))


def _out_proj_kernel(n_in, *refs):
    x_ref, g_ref = refs[0], refs[1]
    o_refs = refs[2:2 + n_in]
    w_refs = refs[2 + n_in:2 + 2 * n_in]
    out_ref = refs[2 + 2 * n_in]
    mix = _dot(o_refs[0][...], w_refs[0][...])
    for o_ref, w_ref in zip(o_refs[1:], w_refs[1:]):
        mix = mix + _dot(o_ref[...], w_ref[...])
    out_ref[...] = x_ref[...] + g_ref[0] * mix


def _out_proj(x, mod, os, ws, seq):
    t = x.shape[0]
    row = _row_spec(D_MODEL)
    return pl.pallas_call(
        functools.partial(_out_proj_kernel, len(os)),
        grid=(t // TILE,),
        in_specs=[row, _mod_spec(seq, 5)] + [_row_spec(o.shape[1]) for o in os] + [_resident(w.shape) for w in ws],
        out_specs=row,
        out_shape=jax.ShapeDtypeStruct(x.shape, F32),
        compiler_params=_params(("parallel",)),
        name="out_proj",
    )(x, mod, *os, *ws)


def _lane_rows(rows):
    width = max(r.shape[0] for r in rows)
    out = jnp.zeros((8, width), F32)
    for i, r in enumerate(rows):
        out = out.at[i, :r.shape[0]].set(r.astype(F32))
    return out


def _block_diag_ones(n, group):
    idx = np.arange(n) // group
    return jnp.asarray((idx[:, None] == idx[None, :]).astype(np.float32), BF16)


def _pad_heads(w, heads, dim):
    r = w.shape[0]
    w = w.reshape(r, heads, dim)
    return jnp.pad(w, ((0, 0), (0, 0), (0, LANES - dim))).reshape(r, heads * LANES)


def kernel(x, c, positions, ada_w, ada_b, ff1_gate, ff1_up, ff1_down, ff2_gate, ff2_up, ff2_down, ab_w_in, mla_w_qb, mla_w_kvb, mla_q_lat_g, mla_kv_lat_g, diff_q_g, diff_k_g, mla_q_g, mla_k_g, diff_lam_q1, diff_lam_k1, diff_lam_q2, diff_lam_k2, diff_subln_g, ab_w_out, fox_w_in, fox_b_f, fox_q_g, fox_k_g, fox_w_out):
    batch, seq, d = x.shape
    depth = ada_w.shape[0]
    t = batch * seq
    assert d == D_MODEL and seq % TILE == 0 and batch <= 8

    c_pad = jnp.zeros((8, d), F32).at[:batch].set(c)
    mod_all = _adaln(c_pad, ada_w, ada_b)

    inv_a = ROPE_THETA ** (-jnp.arange(0, DIFF_HEAD_DIM, 2, dtype=F32) / DIFF_HEAD_DIM)
    inv_b = ROPE_THETA ** (-jnp.arange(0, MLA_ROPE_DIM, 2, dtype=F32) / MLA_ROPE_DIM)
    inv_rows = jnp.zeros((8, LANES), F32).at[0, :32].set(inv_a).at[0, 64:80].set(inv_b)
    lane = np.arange(LANES)
    sgn_a = np.where(lane % 64 < 32, -1.0, 1.0)
    sgn_b = np.where((lane >= 64) & (lane < 80), -1.0, np.where((lane >= 80) & (lane < 96), 1.0, 0.0))
    sgn_rows = jnp.asarray(np.stack([sgn_a, sgn_b] + [np.zeros(LANES)] * 6), F32)
    pos_b = jnp.broadcast_to(positions.astype(F32).reshape(t, 1), (t, LANES))
    tables = _rope_tables(pos_b, inv_rows, sgn_rows)

    g64 = _block_diag_ones(SLAB, 64)
    g128 = _block_diag_ones(SLAB, LANES)
    bf = lambda w: w.astype(BF16)
    ff1 = [_to_bf16(w) for w in (ff1_gate, ff1_up, ff1_down)]
    ff2 = [_to_bf16(w) for w in (ff2_gate, ff2_up, ff2_down)]

    xt = x.reshape(t, d)
    for l in range(depth):
        mod = mod_all[l, :batch].reshape(batch * N_MOD, 1, d)
        xt = _ffn(xt, mod, 0, l, *ff1, seq)
        if l % 2 == 0:
            e = l // 2
            lambda_init = 0.8 - 0.6 * math.exp(-0.3 * l)
            w_in = ab_w_in[e]
            o_cq = 3 * DIFF_W
            o_kr = o_cq + MLA_Q_RANK + MLA_KV_RANK
            kr_pad = jnp.zeros((d, LANES), F32).at[:, MLA_NOPE_DIM:MLA_QK_DIM].set(w_in[:, o_kr:])
            w_all = bf(jnp.concatenate([w_in[:, :o_kr], kr_pad], axis=1))
            wqb = bf(_pad_heads(mla_w_qb[e], MLA_HEADS, MLA_QK_DIM))
            wkv = mla_w_kvb[e].reshape(MLA_KV_RANK, MLA_HEADS, MLA_NOPE_DIM + MLA_V_DIM)
            wkn = bf(_pad_heads(wkv[:, :, :MLA_NOPE_DIM].reshape(MLA_KV_RANK, -1), MLA_HEADS, MLA_NOPE_DIM))
            wvb = bf(wkv[:, :, MLA_NOPE_DIM:].reshape(MLA_KV_RANK, -1))
            pad128 = lambda g: jnp.pad(g, (0, LANES - g.shape[0]))
            gains = _lane_rows([jnp.tile(diff_q_g[e], 4), jnp.tile(diff_k_g[e], 4),
                                jnp.tile(pad128(mla_q_g[e]), 2), jnp.tile(pad128(mla_k_g[e]), 2)])
            qat, ka, vat, qbt, kb, vbt = _ab_proj(
                xt, mod, w_all, wqb, wkn, wvb, gains, mla_q_lat_g[e].reshape(1, -1), mla_kv_lat_g[e].reshape(1, -1),
                g64, g128, tables, batch, seq)
            lam_rows = _lane_rows([pad128(diff_lam_q1[e]), pad128(diff_lam_k1[e]),
                                   pad128(diff_lam_q2[e]), pad128(diff_lam_k2[e])])
            o_a = _diff_attn(qat, ka, vat, lam_rows, diff_subln_g[e].reshape(1, -1), lambda_init, batch, seq)
            o_b = _mla_attn(qbt, kb, vbt, batch, seq)
            w_out = bf(ab_w_out[e])
            xt = _out_proj(xt, mod, [o_a, o_b], [w_out[:DIFF_W], w_out[DIFF_W:]], seq)
        else:
            o = l // 2
            w_fox = bf(jnp.pad(fox_w_in[o], ((0, 0), (0, LANES - FOX_HEADS))))
            gains = _lane_rows([jnp.tile(fox_q_g[o], 4), jnp.tile(fox_k_g[o], 4)])
            bf_row = jnp.pad(fox_b_f[o], (0, LANES - FOX_HEADS)).reshape(1, LANES)
            tri = jnp.asarray(np.tril(np.ones((TILE, TILE), np.float32)), BF16)
            qft, kf, vft, og = _fox_proj(xt, mod, w_fox, gains, bf_row, g64, tri, batch, seq)
            o_f = _fox_attn(qft, kf, vft, og, batch, seq)
            xt = _out_proj(xt, mod, [o_f], [bf(fox_w_out[o])], seq)
        xt = _ffn(xt, mod, 6, l, *ff2, seq)
    return xt.reshape(batch, seq, d)
```

```python
import functools
import math

import numpy as np
import jax
import jax.numpy as jnp
from jax import lax
from jax.experimental import pallas as pl
from jax.experimental.pallas import tpu as pltpu

F32 = jnp.float32
BF16 = jnp.bfloat16

D_MODEL = 1024
D_FF = 2816
N_MOD = 9
ROPE_THETA = 10000.0
EPS = 1e-6
NEG_INF = -1e30
LOG2E = 1.4426950408889634

DIFF_HEADS = 8
DIFF_HEAD_DIM = 64
MLA_HEADS = 8
MLA_NOPE_DIM = 64
MLA_ROPE_DIM = 32
MLA_V_DIM = 128
MLA_Q_RANK = 384
MLA_KV_RANK = 256
FOX_HEADS = 16
FOX_HEAD_DIM = 64

DIFF_W = DIFF_HEADS * 2 * DIFF_HEAD_DIM
MLA_QK_DIM = MLA_NOPE_DIM + MLA_ROPE_DIM
FOX_W = FOX_HEADS * FOX_HEAD_DIM

LANES = 128
SLAB = 256
VMEM_LIMIT = 56 * 1024 * 1024

TILE = 512
FF_CHUNK = 256
TILES_PER_TRIP = 4

FOX_ONES_Q = (64, 65, 66)
FOX_BIAS_Q = (67, 68, 69)


def _params(sem):
    return pltpu.CompilerParams(dimension_semantics=sem, vmem_limit_bytes=VMEM_LIMIT)


def _dot(a, b):
    return jnp.dot(a, b, preferred_element_type=F32)


def _sigmoid(x):
    return 1.0 / (1.0 + jnp.exp(-x))


def _modulated(x, shift, scale):
    ms = jnp.mean(x * x, axis=-1, keepdims=True)
    return x * lax.rsqrt(ms + EPS) * (1.0 + scale) + shift


def _split3(x):
    hi = x.astype(BF16)
    r = x - hi.astype(F32)
    mid = r.astype(BF16)
    lo = (r - mid.astype(F32)).astype(BF16)
    return hi, mid, lo


def _group_sums(sq, gmat):
    return _dot(sq.astype(BF16), gmat)


def _rope(t, cos, sin_signed, half):
    n = t.shape[-1]
    lane = lax.broadcasted_iota(jnp.int32, t.shape, 1)
    up = pltpu.roll(t, n - half, 1)
    dn = pltpu.roll(t, half, 1)
    partner = jnp.where((lane & half) == 0, up, dn)
    return t * cos + partner * sin_signed


def _run_staged(stages):
    pending = stages[0][0]()
    for i, (_, consume) in enumerate(stages):
        ahead = stages[i + 1][0]() if i + 1 < len(stages) else None
        consume(pending)
        pending = ahead


def _put_transposed(ref, first_head, t):
    for j in range(t.shape[1] // LANES):
        ref[0, first_head + j, 0] = t[:, j * LANES:(j + 1) * LANES].T.astype(BF16)


def _cast_kernel(w_ref, o_ref):
    o_ref[...] = w_ref[...].astype(BF16)


def _to_bf16(w):
    depth, rows, cols = w.shape
    rb = 256
    spec = pl.BlockSpec((1, rb, cols), lambda l, i: (l, i, 0))
    return pl.pallas_call(
        _cast_kernel,
        grid=(depth, rows // rb),
        in_specs=[spec],
        out_specs=spec,
        out_shape=jax.ShapeDtypeStruct(w.shape, BF16),
        compiler_params=_params(("parallel", "parallel")),
        name="to_bf16",
    )(w)


def _adaln_kernel(c_ref, w_ref, b_ref, o_ref):
    c = c_ref[...]
    cond = (c * _sigmoid(c)).astype(BF16)
    o_ref[0] = _dot(cond, w_ref[0].astype(BF16)) + b_ref[0]


def _adaln(c_pad, ada_w, ada_b):
    depth, d, n = ada_w.shape
    tn = 1024
    return pl.pallas_call(
        _adaln_kernel,
        grid=(depth, n // tn),
        in_specs=[
            pl.BlockSpec((8, d), lambda l, j: (0, 0)),
            pl.BlockSpec((1, d, tn), lambda l, j: (l, 0, j)),
            pl.BlockSpec((1, 1, tn), lambda l, j: (l, 0, j)),
        ],
        out_specs=pl.BlockSpec((1, 8, tn), lambda l, j: (l, 0, j)),
        out_shape=jax.ShapeDtypeStruct((depth, 8, n), F32),
        compiler_params=_params(("parallel", "parallel")),
        name="adaln",
    )(c_pad, ada_w, ada_b.reshape(depth, 1, n))


def _rope_table_kernel(pos_ref, inv_ref, sgn_ref, cos_a_ref, sin_a_ref, cos_b_ref, sin_b_ref):
    ang = pos_ref[...] * inv_ref[0:1, :]
    c = jnp.cos(ang)
    s = jnp.sin(ang)
    lane = lax.broadcasted_iota(jnp.int32, ang.shape, 1)
    in_a = lane < 32
    in_b = (lane >= 64) & (lane < 80)
    rope_b = (lane >= 64) & (lane < 96)

    def spread_a(t):
        t = jnp.where(in_a, t, 0.0)
        t = t + pltpu.roll(t, 32, 1)
        return t + pltpu.roll(t, 64, 1)

    def spread_b(t):
        t = jnp.where(in_b, t, 0.0)
        return t + pltpu.roll(t, 16, 1)

    cos_a_ref[...] = spread_a(c)
    sin_a_ref[...] = spread_a(s) * sgn_ref[0:1, :]
    cos_b_ref[...] = jnp.where(rope_b, spread_b(c), 1.0)
    sin_b_ref[...] = spread_b(s) * sgn_ref[1:2, :]


def _rope_tables(pos_b, inv_rows, sgn_rows):
    t = pos_b.shape[0]
    ts = TILE
    spec = pl.BlockSpec((ts, LANES), lambda i: (i, 0))
    cspec = pl.BlockSpec((8, LANES), lambda i: (0, 0))
    out = jax.ShapeDtypeStruct((t, LANES), F32)
    return pl.pallas_call(
        _rope_table_kernel,
        grid=(t // ts,),
        in_specs=[spec, cspec, cspec],
        out_specs=[spec, spec, spec, spec],
        out_shape=[out, out, out, out],
        compiler_params=_params(("parallel",)),
        name="rope_tables",
    )(pos_b, inv_rows, sgn_rows)


def _ffn_kernel(x_ref, sh_ref, sc_ref, g_ref, wg_ref, wu_ref, wd_ref, o_ref):
    x = x_ref[...]
    h = _modulated(x, sh_ref[0], sc_ref[0]).astype(BF16)
    acc = jnp.zeros(x.shape, F32)
    for c in range(D_FF // FF_CHUNK):
        cols = slice(c * FF_CHUNK, (c + 1) * FF_CHUNK)
        g = _dot(h, wg_ref[0, :, cols])
        u = _dot(h, wu_ref[0, :, cols])
        a = (g * _sigmoid(g) * u).astype(BF16)
        acc = acc + _dot(a, wd_ref[0, cols, :])
    o_ref[...] = x + (0.5 * g_ref[0]) * acc


def _mod_spec(seq, which):
    return pl.BlockSpec((1, 1, D_MODEL), lambda i: ((i * TILE // seq) * N_MOD + which, 0, 0))


def _resident(shape):
    return pl.BlockSpec(shape, lambda i: (0,) * len(shape), pipeline_mode=pl.Buffered(1))


def _row_spec(width):
    return pl.BlockSpec((TILE, width), lambda i: (i, 0))


def _ffn(x, mod, which, layer, wg, wu, wd, seq):
    t = x.shape[0]
    row = _row_spec(D_MODEL)
    of_layer = lambda w: pl.BlockSpec((1,) + w.shape[1:], lambda i: (layer, 0, 0), pipeline_mode=pl.Buffered(1))
    return pl.pallas_call(
        _ffn_kernel,
        grid=(t // TILE,),
        in_specs=[row, _mod_spec(seq, which), _mod_spec(seq, which + 1), _mod_spec(seq, which + 2),
                  of_layer(wg), of_layer(wu), of_layer(wd)],
        out_specs=row,
        out_shape=jax.ShapeDtypeStruct(x.shape, F32),
        compiler_params=_params(("parallel",)),
        name="ffn",
    )(x, mod, mod, mod, wg, wu, wd)


def _transposed_shape(batch, heads, seq):
    return jax.ShapeDtypeStruct((batch, heads, seq // TILE, LANES, TILE), BF16)


def _transposed_out_spec(heads, seq):
    chunks = seq // TILE
    return pl.BlockSpec((1, heads, 1, LANES, TILE), lambda i: (i // chunks, 0, i % chunks, 0, 0))


def _ab_proj_kernel(x_ref, sh_ref, sc_ref, w_ref, wqb_ref, wkn_ref, wvb_ref, gains_ref, lat_q_ref, lat_kv_ref,
                    g64_ref, g128_ref, cos_a_ref, sin_a_ref, cos_b_ref, sin_b_ref,
                    qat_ref, ka_ref, vat_ref, qbt_ref, kb_ref, vbt_ref):
    x = x_ref[...]
    h = _modulated(x, sh_ref[0], sc_ref[0]).astype(BF16)
    two = lambda r: jnp.concatenate([r, r], axis=1)
    cos_a, sin_a = two(cos_a_ref[...]), two(sin_a_ref[...])
    cos_b, sin_b = two(cos_b_ref[...]), two(sin_b_ref[...])
    gq_a, gk_a = gains_ref[0:1, :], gains_ref[1:2, :]
    gq_b, gk_b = gains_ref[2:3, :], gains_ref[3:4, :]
    g64, g128 = g64_ref[...], g128_ref[...]
    scale_a = DIFF_HEAD_DIM ** -0.5 * LOG2E
    scale_b = MLA_QK_DIM ** -0.5 * LOG2E
    heads_per_slab = SLAB // LANES
    o_cq = 3 * DIFF_W
    o_ckv = o_cq + MLA_Q_RANK
    o_kr = o_ckv + MLA_KV_RANK
    latent = {}

    def head_norm(t, gmat, dim, gain):
        ss = _group_sums(t * t, gmat)
        return t * lax.rsqrt(ss * (1.0 / dim) + EPS) * gain

    def project(cols):
        return lambda: _dot(h, w_ref[:, cols])

    def latent_inputs(y):
        c_q = y[:, :MLA_Q_RANK]
        c_q = c_q * lax.rsqrt(jnp.mean(c_q * c_q, axis=-1, keepdims=True) + EPS) * lat_q_ref[...]
        c_kv = y[:, MLA_Q_RANK:MLA_Q_RANK + MLA_KV_RANK]
        c_kv = c_kv * lax.rsqrt(jnp.mean(c_kv * c_kv, axis=-1, keepdims=True) + EPS) * lat_kv_ref[...]
        latent["q"], latent["kv"] = c_q.astype(BF16), c_kv.astype(BF16)
        latent["k_rope"] = two(y[:, MLA_Q_RANK + MLA_KV_RANK:])

    def diff_q(s):
        def consume(y):
            q = _rope(head_norm(y, g64, DIFF_HEAD_DIM, gq_a), cos_a, sin_a, 32)
            _put_transposed(qat_ref, heads_per_slab * s, q * scale_a)
        return consume

    def diff_k(s):
        def consume(y):
            k = _rope(head_norm(y, g64, DIFF_HEAD_DIM, gk_a), cos_a, sin_a, 32)
            ka_ref[:, s * SLAB:(s + 1) * SLAB] = k.astype(BF16)
        return consume

    def mla_q(s):
        def consume(y):
            q = _rope(head_norm(y, g128, MLA_QK_DIM, gq_b), cos_b, sin_b, 16)
            _put_transposed(qbt_ref, heads_per_slab * s, q * scale_b)
        return consume

    def mla_k(s):
        def consume(y):
            k = _rope(head_norm(y + latent["k_rope"], g128, MLA_QK_DIM, gk_b), cos_b, sin_b, 16)
            kb_ref[:, s * SLAB:(s + 1) * SLAB] = k.astype(BF16)
        return consume

    slab = lambda base, s: slice(base + s * SLAB, base + (s + 1) * SLAB)
    n_slabs = DIFF_W // SLAB
    stages = [(project(slice(o_cq, o_kr + LANES)), latent_inputs)]
    for s in range(n_slabs):
        stages.append((project(slab(0, s)), diff_q(s)))
        stages.append((project(slab(DIFF_W, s)), diff_k(s)))
        stages.append((project(slab(2 * DIFF_W, s)),
                       lambda y, s=s: _put_transposed(vat_ref, heads_per_slab * s, y)))
    for s in range(n_slabs):
        cols = slab(0, s)
        stages.append((lambda cols=cols: _dot(latent["q"], wqb_ref[:, cols]), mla_q(s)))
        stages.append((lambda cols=cols: _dot(latent["kv"], wkn_ref[:, cols]), mla_k(s)))
        stages.append((lambda cols=cols: _dot(latent["kv"], wvb_ref[:, cols]),
                       lambda y, s=s: _put_transposed(vbt_ref, heads_per_slab * s, y)))
    _run_staged(stages)


def _ab_proj(x, mod, w_all, wqb, wkn, wvb, gains, lat_q, lat_kv, g64, g128, tables, batch, seq):
    t = x.shape[0]
    row = _row_spec(D_MODEL)
    tab = _row_spec(LANES)
    rows_out = jax.ShapeDtypeStruct((t, D_MODEL), BF16)
    tr_out = _transposed_shape(batch, DIFF_HEADS, seq)
    tr_spec = _transposed_out_spec(DIFF_HEADS, seq)
    return pl.pallas_call(
        _ab_proj_kernel,
        grid=(t // TILE,),
        in_specs=[row, _mod_spec(seq, 3), _mod_spec(seq, 4),
                  _resident(w_all.shape), _resident(wqb.shape), _resident(wkn.shape), _resident(wvb.shape),
                  _resident(gains.shape), _resident(lat_q.shape), _resident(lat_kv.shape),
                  _resident(g64.shape), _resident(g128.shape), tab, tab, tab, tab],
        out_specs=[tr_spec, row, tr_spec, tr_spec, row, tr_spec],
        out_shape=[tr_out, rows_out, tr_out, tr_out, rows_out, tr_out],
        compiler_params=_params(("parallel",)),
        name="ab_proj",
    )(x, mod, mod, w_all, wqb, wkn, wvb, gains, lat_q, lat_kv, g64, g128, *tables)


def _fox_proj_kernel(seq, x_ref, sh_ref, sc_ref, w_ref, gains_ref, bf_ref, g64_ref, tri_ref,
                     qft_ref, kf_ref, vft_ref, og_ref, carry_ref):
    i = pl.program_id(0)
    tm = x_ref.shape[0]

    @pl.when((i * tm) % seq == 0)
    def _():
        carry_ref[...] = jnp.zeros(carry_ref.shape, F32)

    x = x_ref[...]
    h = _modulated(x, sh_ref[0], sc_ref[0]).astype(BF16)
    lane = lax.broadcasted_iota(jnp.int32, (tm, LANES), 1)
    gq, gk = gains_ref[0:1, :], gains_ref[1:2, :]
    g64 = g64_ref[...]
    scale = FOX_HEAD_DIM ** -0.5 * LOG2E
    heads_per_slab = SLAB // FOX_HEAD_DIM
    gate = {}

    def project(cols):
        return lambda: _dot(h, w_ref[:, cols])

    def forget_cumsum(f):
        z = f + bf_ref[...]
        log_f = jnp.where(lane < FOX_HEADS, jnp.minimum(z, 0.0) - jnp.log1p(jnp.exp(-jnp.abs(z))), 0.0)
        tri = tri_ref[...]
        hi, mid, lo = _split3(log_f)
        cum = _dot(tri, hi) + _dot(tri, mid) + _dot(tri, lo) + carry_ref[0:1, :]
        carry_ref[0:1, :] = cum[tm - 1:tm, :]
        gate["cum"] = cum * LOG2E

    def head_norm(t, gain):
        ss = _group_sums(t * t, g64)
        return t * lax.rsqrt(ss * (1.0 / FOX_HEAD_DIM) + EPS) * gain

    def bias_lanes(pieces, ones_at, pieces_at, sign):
        blk = jnp.zeros((tm, LANES), F32)
        for ln in ones_at:
            blk = jnp.where(lane == ln, 1.0, blk)
        for ln, p in zip(pieces_at, pieces):
            blk = jnp.where(lane == ln, sign * p.astype(F32), blk)
        return blk

    def head_blocks(t, s, is_q):
        for j in range(heads_per_slab):
            hd = s * heads_per_slab + j
            th = t[:, (j // 2) * LANES:(j // 2 + 1) * LANES]
            if j % 2:
                th = pltpu.roll(th, FOX_HEAD_DIM, 1)
            pieces = _split3(jnp.broadcast_to(gate["cum"][:, hd:hd + 1], (tm, LANES)))
            if is_q:
                aug = bias_lanes(pieces, FOX_ONES_Q, FOX_BIAS_Q, 1.0)
                _put_transposed(qft_ref, hd, jnp.where(lane < FOX_HEAD_DIM, th, aug))
            else:
                aug = bias_lanes(pieces, FOX_BIAS_Q, FOX_ONES_Q, -1.0)
                kf_ref[:, hd * LANES:(hd + 1) * LANES] = jnp.where(lane < FOX_HEAD_DIM, th, aug).astype(BF16)

    def store_gate(s):
        def consume(y):
            og_ref[:, s * SLAB:(s + 1) * SLAB] = y
        return consume

    slab = lambda base, s: slice(base + s * SLAB, base + (s + 1) * SLAB)
    stages = [(project(slice(4 * FOX_W, 4 * FOX_W + LANES)), forget_cumsum)]
    for s in range(FOX_W // SLAB):
        stages.append((project(slab(2 * FOX_W, s)),
                       lambda y, s=s: _put_transposed(vft_ref, (SLAB // LANES) * s, y)))
        stages.append((project(slab(3 * FOX_W, s)), store_gate(s)))
    for s in range(FOX_W // SLAB):
        stages.append((project(slab(0, s)), lambda y, s=s: head_blocks(head_norm(y, gq) * scale, s, True)))
        stages.append((project(slab(FOX_W, s)), lambda y, s=s: head_blocks(head_norm(y, gk), s, False)))
    _run_staged(stages)


def _fox_proj(x, mod, w_fox, gains, bf_row, g64, tri, batch, seq):
    t = x.shape[0]
    row = _row_spec(D_MODEL)
    return pl.pallas_call(
        functools.partial(_fox_proj_kernel, seq),
        grid=(t // TILE,),
        in_specs=[row, _mod_spec(seq, 3), _mod_spec(seq, 4), _resident(w_fox.shape),
                  _resident(gains.shape), _resident(bf_row.shape), _resident(g64.shape), _resident(tri.shape)],
        out_specs=[_transposed_out_spec(FOX_HEADS, seq), _row_spec(FOX_HEADS * LANES),
                   _transposed_out_spec(FOX_HEADS // 2, seq), row],
        out_shape=[_transposed_shape(batch, FOX_HEADS, seq), jax.ShapeDtypeStruct((t, FOX_HEADS * LANES), BF16),
                   _transposed_shape(batch, FOX_HEADS // 2, seq), jax.ShapeDtypeStruct((t, FOX_W), F32)],
        scratch_shapes=[pltpu.VMEM((8, LANES), F32)],
        compiler_params=_params(("arbitrary",)),
        name="fox_proj",
    )(x, mod, mod, w_fox, gains, bf_row, g64, tri)


def _tile_order(nq):
    below = [(qi, j) for qi in range(1, nq) for j in range(qi)]
    peeled = len(below) % TILES_PER_TRIP
    unrolled = [(i, i, True) for i in range(nq)] + [(qi, j, False) for qi, j in below[:peeled]]
    looped = below[peeled:]
    table = np.asarray(looped + looped[-1:], np.int32).reshape(-1) if looped else np.zeros((2,), np.int32)
    return unrolled, len(looped), table


def _flash_head(tbl_ref, nq, q_at, k_refs, vt_at, s_buf, m_ref, l_ref, acc_ref):
    n = len(k_refs)
    unrolled, n_looped, _ = _tile_order(nq)

    half = TILE // 2
    lower = lambda t: t[:, :half]
    upper = lambda t: t[:, half:]
    join = lambda a, b: jnp.concatenate([a, b], axis=1)

    def issue_diagonal(qi, slot):
        off = qi * TILE
        key = lax.broadcasted_iota(jnp.int32, (half, half), 0)
        query = lax.broadcasted_iota(jnp.int32, (half, half), 1)
        causal = key <= query
        maxima = []
        for e in range(n):
            qt = q_at[e](qi)
            top = _dot(k_refs[e][off:off + half, :], qt)
            top = join(jnp.where(causal, lower(top), NEG_INF), upper(top))
            bot = jnp.where(causal, _dot(k_refs[e][off + half:off + TILE, :], upper(qt)), NEG_INF)
            s_buf[slot, e, 0:half, :] = top
            s_buf[slot, e, half:TILE, half:TILE] = bot
            top_max = jnp.max(top, axis=0, keepdims=True)
            maxima.append(join(lower(top_max), jnp.maximum(upper(top_max), jnp.max(bot, axis=0, keepdims=True))))
        return tuple(maxima)

    def issue(qi, j, slot, diagonal):
        if diagonal:
            return issue_diagonal(qi, slot)
        off = pl.multiple_of(j * TILE, TILE)
        maxima = []
        for e in range(n):
            s = _dot(k_refs[e][pl.ds(off, TILE), :], q_at[e](qi))
            s_buf[slot, e] = s
            maxima.append(jnp.max(s, axis=0, keepdims=True))
        return tuple(maxima)

    def absorb(qi, j, slot, maxima, first):
        for e in range(n):
            if first:
                m_new = maxima[e]
                p_top = jnp.exp2(s_buf[slot, e, 0:half, :] - m_new)
                p_bot = jnp.exp2(s_buf[slot, e, half:TILE, half:TILE] - upper(m_new))
                l_top = jnp.sum(p_top, axis=0, keepdims=True)
                l = join(lower(l_top), upper(l_top) + jnp.sum(p_bot, axis=0, keepdims=True))
                vt = vt_at[e](j)
                acc_top = _dot(lower(vt), p_top.astype(BF16))
                acc = join(lower(acc_top), upper(acc_top) + _dot(upper(vt), p_bot.astype(BF16)))
            else:
                m = m_ref[e, qi]
                m_new = jnp.maximum(m, maxima[e])
                alpha = jnp.exp2(m - m_new)
                p = jnp.exp2(s_buf[slot, e] - m_new)
                l = alpha * l_ref[e, qi] + jnp.sum(p, axis=0, keepdims=True)
                acc = alpha * acc_ref[e, qi] + _dot(vt_at[e](j), p.astype(BF16))
            m_ref[e, qi] = m_new
            l_ref[e, qi] = l
            acc_ref[e, qi] = acc

    def looped(t):
        return tbl_ref[2 * t], tbl_ref[2 * t + 1]

    pending = issue(*unrolled[0][:2], 0, unrolled[0][2])
    for idx, (qi, j, diagonal) in enumerate(unrolled):
        slot = idx % 2
        ahead = None
        if idx + 1 < len(unrolled):
            nxt = unrolled[idx + 1]
            ahead = issue(nxt[0], nxt[1], 1 - slot, nxt[2])
        elif n_looped:
            ahead = issue(*looped(0), 1 - slot, False)
        absorb(qi, j, slot, pending, diagonal)
        pending = ahead

    if n_looped:
        base = len(unrolled) % 2

        def body(trip, held):
            for k in range(TILES_PER_TRIP):
                t = TILES_PER_TRIP * trip + k
                slot = (base + k) % 2
                ahead = issue(*looped(t + 1), 1 - slot, False)
                absorb(*looped(t), slot, held, False)
                held = ahead
            return held

        lax.fori_loop(0, n_looped // TILES_PER_TRIP, body, pending)

    return lambda e, qi: acc_ref[e, qi] / l_ref[e, qi]


def _diff_attn_kernel(lambda_init, nq, tbl_ref, qt_ref, k_ref, vt_ref, lam_ref, subln_ref, o_ref, *scratch):
    feat = lax.broadcasted_iota(jnp.int32, (LANES, TILE), 0)
    zero = jnp.zeros((LANES, TILE), BF16)
    q1 = lambda qi: jnp.where(feat < DIFF_HEAD_DIM, qt_ref[0, 0, qi], zero)
    q2 = lambda qi: jnp.where(feat < DIFF_HEAD_DIM, zero, qt_ref[0, 0, qi])
    vt = lambda j: vt_ref[0, 0, j]
    out_t = _flash_head(tbl_ref, nq, [q1, q2], [k_ref, k_ref], [vt, vt], *scratch)
    lv = lam_ref[...]
    lam = (jnp.exp(jnp.sum(lv[0:1] * lv[1:2], axis=-1, keepdims=True))
           - jnp.exp(jnp.sum(lv[2:3] * lv[3:4], axis=-1, keepdims=True)) + lambda_init)
    for qi in range(nq):
        o = (out_t(0, qi) - lam * out_t(1, qi)).T
        o = o * lax.rsqrt(jnp.mean(o * o, axis=-1, keepdims=True) + EPS) * subln_ref[...]
        o_ref[qi * TILE:(qi + 1) * TILE, :] = (o * (1.0 - lambda_init)).astype(BF16)


def _mla_attn_kernel(nq, tbl_ref, qt_ref, k0_ref, k1_ref, vt_ref, o_ref, *scratch):
    out_t = _flash_head(tbl_ref, nq, [lambda qi: qt_ref[0, 0, qi], lambda qi: qt_ref[0, 1, qi]], [k0_ref, k1_ref],
                        [lambda j: vt_ref[0, 0, j], lambda j: vt_ref[0, 1, j]], *scratch)
    for qi in range(nq):
        for e in range(2):
            o_ref[qi * TILE:(qi + 1) * TILE, e * LANES:(e + 1) * LANES] = out_t(e, qi).T.astype(BF16)


def _fox_attn_kernel(nq, tbl_ref, qt_ref, k0_ref, k1_ref, vt_ref, og_ref, o_ref, *scratch):
    vt = [lambda j, e=e: vt_ref[0, 0, j, e * FOX_HEAD_DIM:(e + 1) * FOX_HEAD_DIM, :] for e in range(2)]
    out_t = _flash_head(tbl_ref, nq, [lambda qi: qt_ref[0, 0, qi], lambda qi: qt_ref[0, 1, qi]], [k0_ref, k1_ref],
                        vt, *scratch)
    for qi in range(nq):
        rows = slice(qi * TILE, (qi + 1) * TILE)
        o = jnp.concatenate([out_t(0, qi), out_t(1, qi)], axis=0).T
        o_ref[rows, :] = (o * _sigmoid(og_ref[rows, :])).astype(BF16)


def _head_tiles_spec(heads, seq):
    return pl.BlockSpec((1, heads, seq // TILE, LANES, TILE), lambda b, h, tbl: (b, h, 0, 0, 0))


def _head_rows_spec(seq, width, head_of=lambda h: h):
    return pl.BlockSpec((seq, width), lambda b, h, tbl: (b, head_of(h)))


def _attn_call(body, name, batch, groups, seq, in_specs, out_spec, out_shape, operands, value_rows=LANES):
    nq = seq // TILE
    table = jnp.asarray(_tile_order(nq)[2])
    stat = pltpu.VMEM((2, nq, 1, TILE), F32)
    return pl.pallas_call(
        functools.partial(body, nq),
        grid_spec=pltpu.PrefetchScalarGridSpec(
            num_scalar_prefetch=1,
            grid=(batch, groups),
            in_specs=in_specs,
            out_specs=out_spec,
            scratch_shapes=[pltpu.VMEM((2, 2, TILE, TILE), F32),
                            stat, stat, pltpu.VMEM((2, nq, value_rows, TILE), F32)],
        ),
        out_shape=out_shape,
        compiler_params=_params(("parallel", "parallel")),
        name=name,
    )(table, *operands)


def _diff_attn(qat, ka, vat, lam_rows, subln, lambda_init, batch, seq):
    const = lambda shape: pl.BlockSpec(shape, lambda b, h, tbl: (0, 0))
    return _attn_call(
        functools.partial(_diff_attn_kernel, lambda_init), "diff_attn", batch, DIFF_HEADS, seq,
        [_head_tiles_spec(1, seq), _head_rows_spec(seq, LANES), _head_tiles_spec(1, seq),
         const(lam_rows.shape), const(subln.shape)],
        _head_rows_spec(seq, LANES), jax.ShapeDtypeStruct(ka.shape, BF16), (qat, ka, vat, lam_rows, subln))


def _mla_attn(qbt, kb, vbt, batch, seq):
    return _attn_call(
        _mla_attn_kernel, "mla_attn", batch, MLA_HEADS // 2, seq,
        [_head_tiles_spec(2, seq), _head_rows_spec(seq, LANES, lambda p: 2 * p),
         _head_rows_spec(seq, LANES, lambda p: 2 * p + 1), _head_tiles_spec(2, seq)],
        _head_rows_spec(seq, 2 * LANES), jax.ShapeDtypeStruct(kb.shape, BF16), (qbt, kb, kb, vbt))


def _fox_attn(qft, kf, vft, og, batch, seq):
    return _attn_call(
        _fox_attn_kernel, "fox_attn", batch, FOX_HEADS // 2, seq,
        [_head_tiles_spec(2, seq), _head_rows_spec(seq, LANES, lambda p: 2 * p),
         _head_rows_spec(seq, LANES, lambda p: 2 * p + 1), _head_tiles_spec(1, seq), _head_rows_spec(seq, LANES)],
        _head_rows_spec(seq, LANES), jax.ShapeDtypeStruct(og.shape, BF16), (qft, kf, kf, vft, og),
        value_rows=FOX_HEAD_DIM)


def _out_proj_kernel(n_in, *refs):
    x_ref, g_ref = refs[0], refs[1]
    o_refs = refs[2:2 + n_in]
    w_refs = refs[2 + n_in:2 + 2 * n_in]
    out_ref = refs[2 + 2 * n_in]
    mix = _dot(o_refs[0][...], w_refs[0][...])
    for o_ref, w_ref in zip(o_refs[1:], w_refs[1:]):
        mix = mix + _dot(o_ref[...], w_ref[...])
    out_ref[...] = x_ref[...] + g_ref[0] * mix


def _out_proj(x, mod, os, ws, seq):
    t = x.shape[0]
    row = _row_spec(D_MODEL)
    return pl.pallas_call(
        functools.partial(_out_proj_kernel, len(os)),
        grid=(t // TILE,),
        in_specs=[row, _mod_spec(seq, 5)] + [_row_spec(o.shape[1]) for o in os] + [_resident(w.shape) for w in ws],
        out_specs=row,
        out_shape=jax.ShapeDtypeStruct(x.shape, F32),
        compiler_params=_params(("parallel",)),
        name="out_proj",
    )(x, mod, *os, *ws)


def _lane_rows(rows):
    width = max(r.shape[0] for r in rows)
    out = jnp.zeros((8, width), F32)
    for i, r in enumerate(rows):
        out = out.at[i, :r.shape[0]].set(r.astype(F32))
    return out


def _block_diag_ones(n, group):
    idx = np.arange(n) // group
    return jnp.asarray((idx[:, None] == idx[None, :]).astype(np.float32), BF16)


def _pad_heads(w, heads, dim):
    r = w.shape[0]
    w = w.reshape(r, heads, dim)
    return jnp.pad(w, ((0, 0), (0, 0), (0, LANES - dim))).reshape(r, heads * LANES)


def kernel(x, c, positions, ada_w, ada_b, ff1_gate, ff1_up, ff1_down, ff2_gate, ff2_up, ff2_down, ab_w_in, mla_w_qb, mla_w_kvb, mla_q_lat_g, mla_kv_lat_g, diff_q_g, diff_k_g, mla_q_g, mla_k_g, diff_lam_q1, diff_lam_k1, diff_lam_q2, diff_lam_k2, diff_subln_g, ab_w_out, fox_w_in, fox_b_f, fox_q_g, fox_k_g, fox_w_out):
    batch, seq, d = x.shape
    depth = ada_w.shape[0]
    t = batch * seq
    assert d == D_MODEL and seq % TILE == 0 and batch <= 8

    c_pad = jnp.zeros((8, d), F32).at[:batch].set(c)
    mod_all = _adaln(c_pad, ada_w, ada_b)

    inv_a = ROPE_THETA ** (-jnp.arange(0, DIFF_HEAD_DIM, 2, dtype=F32) / DIFF_HEAD_DIM)
    inv_b = ROPE_THETA ** (-jnp.arange(0, MLA_ROPE_DIM, 2, dtype=F32) / MLA_ROPE_DIM)
    inv_rows = jnp.zeros((8, LANES), F32).at[0, :32].set(inv_a).at[0, 64:80].set(inv_b)
    lane = np.arange(LANES)
    sgn_a = np.where(lane % 64 < 32, -1.0, 1.0)
    sgn_b = np.where((lane >= 64) & (lane < 80), -1.0, np.where((lane >= 80) & (lane < 96), 1.0, 0.0))
    sgn_rows = jnp.asarray(np.stack([sgn_a, sgn_b] + [np.zeros(LANES)] * 6), F32)
    pos_b = jnp.broadcast_to(positions.astype(F32).reshape(t, 1), (t, LANES))
    tables = _rope_tables(pos_b, inv_rows, sgn_rows)

    g64 = _block_diag_ones(SLAB, 64)
    g128 = _block_diag_ones(SLAB, LANES)
    bf = lambda w: w.astype(BF16)
    ff1 = [_to_bf16(w) for w in (ff1_gate, ff1_up, ff1_down)]
    ff2 = [_to_bf16(w) for w in (ff2_gate, ff2_up, ff2_down)]

    xt = x.reshape(t, d)
    for l in range(depth):
        mod = mod_all[l, :batch].reshape(batch * N_MOD, 1, d)
        xt = _ffn(xt, mod, 0, l, *ff1, seq)
        if l % 2 == 0:
            e = l // 2
            lambda_init = 0.8 - 0.6 * math.exp(-0.3 * l)
            w_in = ab_w_in[e]
            o_cq = 3 * DIFF_W
            o_kr = o_cq + MLA_Q_RANK + MLA_KV_RANK
            kr_pad = jnp.zeros((d, LANES), F32).at[:, MLA_NOPE_DIM:MLA_QK_DIM].set(w_in[:, o_kr:])
            w_all = bf(jnp.concatenate([w_in[:, :o_kr], kr_pad], axis=1))
            wqb = bf(_pad_heads(mla_w_qb[e], MLA_HEADS, MLA_QK_DIM))
            wkv = mla_w_kvb[e].reshape(MLA_KV_RANK, MLA_HEADS, MLA_NOPE_DIM + MLA_V_DIM)
            wkn = bf(_pad_heads(wkv[:, :, :MLA_NOPE_DIM].reshape(MLA_KV_RANK, -1), MLA_HEADS, MLA_NOPE_DIM))
            wvb = bf(wkv[:, :, MLA_NOPE_DIM:].reshape(MLA_KV_RANK, -1))
            pad128 = lambda g: jnp.pad(g, (0, LANES - g.shape[0]))
            gains = _lane_rows([jnp.tile(diff_q_g[e], 4), jnp.tile(diff_k_g[e], 4),
                                jnp.tile(pad128(mla_q_g[e]), 2), jnp.tile(pad128(mla_k_g[e]), 2)])
            qat, ka, vat, qbt, kb, vbt = _ab_proj(
                xt, mod, w_all, wqb, wkn, wvb, gains, mla_q_lat_g[e].reshape(1, -1), mla_kv_lat_g[e].reshape(1, -1),
                g64, g128, tables, batch, seq)
            lam_rows = _lane_rows([pad128(diff_lam_q1[e]), pad128(diff_lam_k1[e]),
                                   pad128(diff_lam_q2[e]), pad128(diff_lam_k2[e])])
            o_a = _diff_attn(qat, ka, vat, lam_rows, diff_subln_g[e].reshape(1, -1), lambda_init, batch, seq)
            o_b = _mla_attn(qbt, kb, vbt, batch, seq)
            w_out = bf(ab_w_out[e])
            xt = _out_proj(xt, mod, [o_a, o_b], [w_out[:DIFF_W], w_out[DIFF_W:]], seq)
        else:
            o = l // 2
            w_fox = bf(jnp.pad(fox_w_in[o], ((0, 0), (0, LANES - FOX_HEADS))))
            gains = _lane_rows([jnp.tile(fox_q_g[o], 4), jnp.tile(fox_k_g[o], 4)])
            bf_row = jnp.pad(fox_b_f[o], (0, LANES - FOX_HEADS)).reshape(1, LANES)
            tri = jnp.asarray(np.tril(np.ones((TILE, TILE), np.float32)), BF16)
            qft, kf, vft, og = _fox_proj(xt, mod, w_fox, gains, bf_row, g64, tri, batch, seq)
            o_f = _fox_attn(qft, kf, vft, og, batch, seq)
            xt = _out_proj(xt, mod, [o_f], [bf(fox_w_out[o])], seq)
        xt = _ffn(xt, mod, 6, l, *ff2, seq)
    return xt.reshape(batch, seq, d)
```

```python
import functools
import math

import numpy as np
import jax
import jax.numpy as jnp
from jax import lax
from jax.experimental import pallas as pl
from jax.experimental.pallas import tpu as pltpu

F32 = jnp.float32
BF16 = jnp.bfloat16

D_MODEL = 1024
D_FF = 2816
N_MOD = 9
ROPE_THETA = 10000.0
EPS = 1e-6
NEG_INF = -1e30
LOG2E = 1.4426950408889634

DIFF_HEADS = 8
DIFF_HEAD_DIM = 64
MLA_HEADS = 8
MLA_NOPE_DIM = 64
MLA_ROPE_DIM = 32
MLA_V_DIM = 128
MLA_Q_RANK = 384
MLA_KV_RANK = 256
FOX_HEADS = 16
FOX_HEAD_DIM = 64

DIFF_W = DIFF_HEADS * 2 * DIFF_HEAD_DIM
MLA_QK_DIM = MLA_NOPE_DIM + MLA_ROPE_DIM
FOX_W = FOX_HEADS * FOX_HEAD_DIM

LANES = 128
SLAB = 256
VMEM_LIMIT = 56 * 1024 * 1024

TILE = 512
FF_CHUNK = 256
TILES_PER_TRIP = 4

FOX_ONES_Q = (64, 65, 66)
FOX_BIAS_Q = (67, 68, 69)


def _params(sem):
    return pltpu.CompilerParams(dimension_semantics=sem, vmem_limit_bytes=VMEM_LIMIT)


def _dot(a, b):
    return jnp.dot(a, b, preferred_element_type=F32)


def _sigmoid(x):
    return 1.0 / (1.0 + jnp.exp(-x))


def _modulated(x, shift, scale):
    ms = jnp.mean(x * x, axis=-1, keepdims=True)
    return x * lax.rsqrt(ms + EPS) * (1.0 + scale) + shift


def _split3(x):
    hi = x.astype(BF16)
    r = x - hi.astype(F32)
    mid = r.astype(BF16)
    lo = (r - mid.astype(F32)).astype(BF16)
    return hi, mid, lo


def _group_sums(sq, gmat):
    return _dot(sq.astype(BF16), gmat)


def _rope(t, cos, sin_signed, half):
    n = t.shape[-1]
    lane = lax.broadcasted_iota(jnp.int32, t.shape, 1)
    up = pltpu.roll(t, n - half, 1)
    dn = pltpu.roll(t, half, 1)
    partner = jnp.where((lane & half) == 0, up, dn)
    return t * cos + partner * sin_signed


def _run_staged(stages):
    pending = stages[0][0]()
    for i, (_, consume) in enumerate(stages):
        ahead = stages[i + 1][0]() if i + 1 < len(stages) else None
        consume(pending)
        pending = ahead


def _put_transposed(ref, first_head, t):
    for j in range(t.shape[1] // LANES):
        ref[0, first_head + j, 0] = t[:, j * LANES:(j + 1) * LANES].T.astype(BF16)


def _cast_kernel(w_ref, o_ref):
    o_ref[...] = w_ref[...].astype(BF16)


def _to_bf16(w):
    depth, rows, cols = w.shape
    rb = 256
    spec = pl.BlockSpec((1, rb, cols), lambda l, i: (l, i, 0))
    return pl.pallas_call(
        _cast_kernel,
        grid=(depth, rows // rb),
        in_specs=[spec],
        out_specs=spec,
        out_shape=jax.ShapeDtypeStruct(w.shape, BF16),
        compiler_params=_params(("parallel", "parallel")),
        name="to_bf16",
    )(w)


def _adaln_kernel(c_ref, w_ref, b_ref, o_ref):
    c = c_ref[...]
    cond = (c * _sigmoid(c)).astype(BF16)
    o_ref[0] = _dot(cond, w_ref[0].astype(BF16)) + b_ref[0]


def _adaln(c_pad, ada_w, ada_b):
    depth, d, n = ada_w.shape
    tn = 1024
    return pl.pallas_call(
        _adaln_kernel,
        grid=(depth, n // tn),
        in_specs=[
            pl.BlockSpec((8, d), lambda l, j: (0, 0)),
            pl.BlockSpec((1, d, tn), lambda l, j: (l, 0, j)),
            pl.BlockSpec((1, 1, tn), lambda l, j: (l, 0, j)),
        ],
        out_specs=pl.BlockSpec((1, 8, tn), lambda l, j: (l, 0, j)),
        out_shape=jax.ShapeDtypeStruct((depth, 8, n), F32),
        compiler_params=_params(("parallel", "parallel")),
        name="adaln",
    )(c_pad, ada_w, ada_b.reshape(depth, 1, n))


def _rope_table_kernel(pos_ref, inv_ref, sgn_ref, cos_a_ref, sin_a_ref, cos_b_ref, sin_b_ref):
    ang = pos_ref[...] * inv_ref[0:1, :]
    c = jnp.cos(ang)
    s = jnp.sin(ang)
    lane = lax.broadcasted_iota(jnp.int32, ang.shape, 1)
    in_a = lane < 32
    in_b = (lane >= 64) & (lane < 80)
    rope_b = (lane >= 64) & (lane < 96)

    def spread_a(t):
        t = jnp.where(in_a, t, 0.0)
        t = t + pltpu.roll(t, 32, 1)
        return t + pltpu.roll(t, 64, 1)

    def spread_b(t):
        t = jnp.where(in_b, t, 0.0)
        return t + pltpu.roll(t, 16, 1)

    cos_a_ref[...] = spread_a(c)
    sin_a_ref[...] = spread_a(s) * sgn_ref[0:1, :]
    cos_b_ref[...] = jnp.where(rope_b, spread_b(c), 1.0)
    sin_b_ref[...] = spread_b(s) * sgn_ref[1:2, :]


def _rope_tables(pos_b, inv_rows, sgn_rows):
    t = pos_b.shape[0]
    ts = TILE
    spec = pl.BlockSpec((ts, LANES), lambda i: (i, 0))
    cspec = pl.BlockSpec((8, LANES), lambda i: (0, 0))
    out = jax.ShapeDtypeStruct((t, LANES), F32)
    return pl.pallas_call(
        _rope_table_kernel,
        grid=(t // ts,),
        in_specs=[spec, cspec, cspec],
        out_specs=[spec, spec, spec, spec],
        out_shape=[out, out, out, out],
        compiler_params=_params(("parallel",)),
        name="rope_tables",
    )(pos_b, inv_rows, sgn_rows)


def _ffn_kernel(x_ref, sh_ref, sc_ref, g_ref, wg_ref, wu_ref, wd_ref, o_ref):
    x = x_ref[...]
    h = _modulated(x, sh_ref[0], sc_ref[0]).astype(BF16)
    acc = jnp.zeros(x.shape, F32)
    for c in range(D_FF // FF_CHUNK):
        cols = slice(c * FF_CHUNK, (c + 1) * FF_CHUNK)
        g = _dot(h, wg_ref[0, :, cols])
        u = _dot(h, wu_ref[0, :, cols])
        a = (g * _sigmoid(g) * u).astype(BF16)
        acc = acc + _dot(a, wd_ref[0, cols, :])
    o_ref[...] = x + (0.5 * g_ref[0]) * acc


def _mod_spec(seq, which):
    return pl.BlockSpec((1, 1, D_MODEL), lambda i: ((i * TILE // seq) * N_MOD + which, 0, 0))


def _resident(shape):
    return pl.BlockSpec(shape, lambda i: (0,) * len(shape), pipeline_mode=pl.Buffered(1))


def _row_spec(width):
    return pl.BlockSpec((TILE, width), lambda i: (i, 0))


def _ffn(x, mod, which, layer, wg, wu, wd, seq):
    t = x.shape[0]
    row = _row_spec(D_MODEL)
    of_layer = lambda w: pl.BlockSpec((1,) + w.shape[1:], lambda i: (layer, 0, 0), pipeline_mode=pl.Buffered(1))
    return pl.pallas_call(
        _ffn_kernel,
        grid=(t // TILE,),
        in_specs=[row, _mod_spec(seq, which), _mod_spec(seq, which + 1), _mod_spec(seq, which + 2),
                  of_layer(wg), of_layer(wu), of_layer(wd)],
        out_specs=row,
        out_shape=jax.ShapeDtypeStruct(x.shape, F32),
        compiler_params=_params(("parallel",)),
        name="ffn",
    )(x, mod, mod, mod, wg, wu, wd)


def _transposed_shape(batch, heads, seq):
    return jax.ShapeDtypeStruct((batch, heads, seq // TILE, LANES, TILE), BF16)


def _transposed_out_spec(heads, seq):
    chunks = seq // TILE
    return pl.BlockSpec((1, heads, 1, LANES, TILE), lambda i: (i // chunks, 0, i % chunks, 0, 0))


def _ab_proj_kernel(x_ref, sh_ref, sc_ref, w_ref, wqb_ref, wkn_ref, wvb_ref, gains_ref, lat_q_ref, lat_kv_ref,
                    g64_ref, g128_ref, cos_a_ref, sin_a_ref, cos_b_ref, sin_b_ref,
                    qat_ref, ka_ref, vat_ref, qbt_ref, kb_ref, vbt_ref):
    x = x_ref[...]
    h = _modulated(x, sh_ref[0], sc_ref[0]).astype(BF16)
    two = lambda r: jnp.concatenate([r, r], axis=1)
    cos_a, sin_a = two(cos_a_ref[...]), two(sin_a_ref[...])
    cos_b, sin_b = two(cos_b_ref[...]), two(sin_b_ref[...])
    gq_a, gk_a = gains_ref[0:1, :], gains_ref[1:2, :]
    gq_b, gk_b = gains_ref[2:3, :], gains_ref[3:4, :]
    g64, g128 = g64_ref[...], g128_ref[...]
    scale_a = DIFF_HEAD_DIM ** -0.5 * LOG2E
    scale_b = MLA_QK_DIM ** -0.5 * LOG2E
    heads_per_slab = SLAB // LANES
    o_cq = 3 * DIFF_W
    o_ckv = o_cq + MLA_Q_RANK
    o_kr = o_ckv + MLA_KV_RANK
    latent = {}

    def head_norm(t, gmat, dim, gain):
        ss = _group_sums(t * t, gmat)
        return t * lax.rsqrt(ss * (1.0 / dim) + EPS) * gain

    def project(cols):
        return lambda: _dot(h, w_ref[:, cols])

    def latent_inputs(y):
        c_q = y[:, :MLA_Q_RANK]
        c_q = c_q * lax.rsqrt(jnp.mean(c_q * c_q, axis=-1, keepdims=True) + EPS) * lat_q_ref[...]
        c_kv = y[:, MLA_Q_RANK:MLA_Q_RANK + MLA_KV_RANK]
        c_kv = c_kv * lax.rsqrt(jnp.mean(c_kv * c_kv, axis=-1, keepdims=True) + EPS) * lat_kv_ref[...]
        latent["q"], latent["kv"] = c_q.astype(BF16), c_kv.astype(BF16)
        latent["k_rope"] = two(y[:, MLA_Q_RANK + MLA_KV_RANK:])

    def diff_q(s):
        def consume(y):
            q = _rope(head_norm(y, g64, DIFF_HEAD_DIM, gq_a), cos_a, sin_a, 32)
            _put_transposed(qat_ref, heads_per_slab * s, q * scale_a)
        return consume

    def diff_k(s):
        def consume(y):
            k = _rope(head_norm(y, g64, DIFF_HEAD_DIM, gk_a), cos_a, sin_a, 32)
            ka_ref[:, s * SLAB:(s + 1) * SLAB] = k.astype(BF16)
        return consume

    def mla_q(s):
        def consume(y):
            q = _rope(head_norm(y, g128, MLA_QK_DIM, gq_b), cos_b, sin_b, 16)
            _put_transposed(qbt_ref, heads_per_slab * s, q * scale_b)
        return consume

    def mla_k(s):
        def consume(y):
            k = _rope(head_norm(y + latent["k_rope"], g128, MLA_QK_DIM, gk_b), cos_b, sin_b, 16)
            kb_ref[:, s * SLAB:(s + 1) * SLAB] = k.astype(BF16)
        return consume

    slab = lambda base, s: slice(base + s * SLAB, base + (s + 1) * SLAB)
    n_slabs = DIFF_W // SLAB
    stages = [(project(slice(o_cq, o_kr + LANES)), latent_inputs)]
    for s in range(n_slabs):
        stages.append((project(slab(0, s)), diff_q(s)))
        stages.append((project(slab(DIFF_W, s)), diff_k(s)))
        stages.append((project(slab(2 * DIFF_W, s)),
                       lambda y, s=s: _put_transposed(vat_ref, heads_per_slab * s, y)))
    for s in range(n_slabs):
        cols = slab(0, s)
        stages.append((lambda cols=cols: _dot(latent["q"], wqb_ref[:, cols]), mla_q(s)))
        stages.append((lambda cols=cols: _dot(latent["kv"], wkn_ref[:, cols]), mla_k(s)))
        stages.append((lambda cols=cols: _dot(latent["kv"], wvb_ref[:, cols]),
                       lambda y, s=s: _put_transposed(vbt_ref, heads_per_slab * s, y)))
    _run_staged(stages)


def _ab_proj(x, mod, w_all, wqb, wkn, wvb, gains, lat_q, lat_kv, g64, g128, tables, batch, seq):
    t = x.shape[0]
    row = _row_spec(D_MODEL)
    tab = _row_spec(LANES)
    rows_out = jax.ShapeDtypeStruct((t, D_MODEL), BF16)
    tr_out = _transposed_shape(batch, DIFF_HEADS, seq)
    tr_spec = _transposed_out_spec(DIFF_HEADS, seq)
    return pl.pallas_call(
        _ab_proj_kernel,
        grid=(t // TILE,),
        in_specs=[row, _mod_spec(seq, 3), _mod_spec(seq, 4),
                  _resident(w_all.shape), _resident(wqb.shape), _resident(wkn.shape), _resident(wvb.shape),
                  _resident(gains.shape), _resident(lat_q.shape), _resident(lat_kv.shape),
                  _resident(g64.shape), _resident(g128.shape), tab, tab, tab, tab],
        out_specs=[tr_spec, row, tr_spec, tr_spec, row, tr_spec],
        out_shape=[tr_out, rows_out, tr_out, tr_out, rows_out, tr_out],
        compiler_params=_params(("parallel",)),
        name="ab_proj",
    )(x, mod, mod, w_all, wqb, wkn, wvb, gains, lat_q, lat_kv, g64, g128, *tables)


def _fox_proj_kernel(seq, x_ref, sh_ref, sc_ref, w_ref, gains_ref, bf_ref, g64_ref, tri_ref,
                     qft_ref, kf_ref, vft_ref, og_ref, carry_ref):
    i = pl.program_id(0)
    tm = x_ref.shape[0]

    @pl.when((i * tm) % seq == 0)
    def _():
        carry_ref[...] = jnp.zeros(carry_ref.shape, F32)

    x = x_ref[...]
    h = _modulated(x, sh_ref[0], sc_ref[0]).astype(BF16)
    lane = lax.broadcasted_iota(jnp.int32, (tm, LANES), 1)
    gq, gk = gains_ref[0:1, :], gains_ref[1:2, :]
    g64 = g64_ref[...]
    scale = FOX_HEAD_DIM ** -0.5 * LOG2E
    heads_per_slab = SLAB // FOX_HEAD_DIM
    gate = {}

    def project(cols):
        return lambda: _dot(h, w_ref[:, cols])

    def forget_cumsum(f):
        z = f + bf_ref[...]
        log_f = jnp.where(lane < FOX_HEADS, jnp.minimum(z, 0.0) - jnp.log1p(jnp.exp(-jnp.abs(z))), 0.0)
        tri = tri_ref[...]
        hi, mid, lo = _split3(log_f)
        cum = _dot(tri, hi) + _dot(tri, mid) + _dot(tri, lo) + carry_ref[0:1, :]
        carry_ref[0:1, :] = cum[tm - 1:tm, :]
        gate["cum"] = cum * LOG2E

    def head_norm(t, gain):
        ss = _group_sums(t * t, g64)
        return t * lax.rsqrt(ss * (1.0 / FOX_HEAD_DIM) + EPS) * gain

    def bias_lanes(pieces, ones_at, pieces_at, sign):
        blk = jnp.zeros((tm, LANES), F32)
        for ln in ones_at:
            blk = jnp.where(lane == ln, 1.0, blk)
        for ln, p in zip(pieces_at, pieces):
            blk = jnp.where(lane == ln, sign * p.astype(F32), blk)
        return blk

    def head_blocks(t, s, is_q):
        for j in range(heads_per_slab):
            hd = s * heads_per_slab + j
            th = t[:, (j // 2) * LANES:(j // 2 + 1) * LANES]
            if j % 2:
                th = pltpu.roll(th, FOX_HEAD_DIM, 1)
            pieces = _split3(jnp.broadcast_to(gate["cum"][:, hd:hd + 1], (tm, LANES)))
            if is_q:
                aug = bias_lanes(pieces, FOX_ONES_Q, FOX_BIAS_Q, 1.0)
                _put_transposed(qft_ref, hd, jnp.where(lane < FOX_HEAD_DIM, th, aug))
            else:
                aug = bias_lanes(pieces, FOX_BIAS_Q, FOX_ONES_Q, -1.0)
                kf_ref[:, hd * LANES:(hd + 1) * LANES] = jnp.where(lane < FOX_HEAD_DIM, th, aug).astype(BF16)

    def store_gate(s):
        def consume(y):
            og_ref[:, s * SLAB:(s + 1) * SLAB] = y
        return consume

    slab = lambda base, s: slice(base + s * SLAB, base + (s + 1) * SLAB)
    stages = [(project(slice(4 * FOX_W, 4 * FOX_W + LANES)), forget_cumsum)]
    for s in range(FOX_W // SLAB):
        stages.append((project(slab(2 * FOX_W, s)),
                       lambda y, s=s: _put_transposed(vft_ref, (SLAB // LANES) * s, y)))
        stages.append((project(slab(3 * FOX_W, s)), store_gate(s)))
    for s in range(FOX_W // SLAB):
        stages.append((project(slab(0, s)), lambda y, s=s: head_blocks(head_norm(y, gq) * scale, s, True)))
        stages.append((project(slab(FOX_W, s)), lambda y, s=s: head_blocks(head_norm(y, gk), s, False)))
    _run_staged(stages)


def _fox_proj(x, mod, w_fox, gains, bf_row, g64, tri, batch, seq):
    t = x.shape[0]
    row = _row_spec(D_MODEL)
    return pl.pallas_call(
        functools.partial(_fox_proj_kernel, seq),
        grid=(t // TILE,),
        in_specs=[row, _mod_spec(seq, 3), _mod_spec(seq, 4), _resident(w_fox.shape),
                  _resident(gains.shape), _resident(bf_row.shape), _resident(g64.shape), _resident(tri.shape)],
        out_specs=[_transposed_out_spec(FOX_HEADS, seq), _row_spec(FOX_HEADS * LANES),
                   _transposed_out_spec(FOX_HEADS // 2, seq), row],
        out_shape=[_transposed_shape(batch, FOX_HEADS, seq), jax.ShapeDtypeStruct((t, FOX_HEADS * LANES), BF16),
                   _transposed_shape(batch, FOX_HEADS // 2, seq), jax.ShapeDtypeStruct((t, FOX_W), F32)],
        scratch_shapes=[pltpu.VMEM((8, LANES), F32)],
        compiler_params=_params(("arbitrary",)),
        name="fox_proj",
    )(x, mod, mod, w_fox, gains, bf_row, g64, tri)


def _tile_order(nq):
    below = [(qi, j) for qi in range(1, nq) for j in range(qi)]
    peeled = len(below) % TILES_PER_TRIP
    unrolled = [(i, i, True) for i in range(nq)] + [(qi, j, False) for qi, j in below[:peeled]]
    looped = below[peeled:]
    table = np.asarray(looped + looped[-1:], np.int32).reshape(-1) if looped else np.zeros((2,), np.int32)
    return unrolled, len(looped), table


def _flash_head(tbl_ref, nq, q_at, k_refs, vt_at, s_buf, m_ref, l_ref, acc_ref):
    n = len(k_refs)
    unrolled, n_looped, _ = _tile_order(nq)

    half = TILE // 2
    lower = lambda t: t[:, :half]
    upper = lambda t: t[:, half:]
    join = lambda a, b: jnp.concatenate([a, b], axis=1)

    def issue_diagonal(qi, slot):
        off = qi * TILE
        key = lax.broadcasted_iota(jnp.int32, (half, half), 0)
        query = lax.broadcasted_iota(jnp.int32, (half, half), 1)
        causal = key <= query
        maxima = []
        for e in range(n):
            qt = q_at[e](qi)
            top = _dot(k_refs[e][off:off + half, :], qt)
            top = join(jnp.where(causal, lower(top), NEG_INF), upper(top))
            bot = jnp.where(causal, _dot(k_refs[e][off + half:off + TILE, :], upper(qt)), NEG_INF)
            s_buf[slot, e, 0:half, :] = top
            s_buf[slot, e, half:TILE, half:TILE] = bot
            top_max = jnp.max(top, axis=0, keepdims=True)
            maxima.append(join(lower(top_max), jnp.maximum(upper(top_max), jnp.max(bot, axis=0, keepdims=True))))
        return tuple(maxima)

    def issue(qi, j, slot, diagonal):
        if diagonal:
            return issue_diagonal(qi, slot)
        off = pl.multiple_of(j * TILE, TILE)
        maxima = []
        for e in range(n):
            s = _dot(k_refs[e][pl.ds(off, TILE), :], q_at[e](qi))
            s_buf[slot, e] = s
            maxima.append(jnp.max(s, axis=0, keepdims=True))
        return tuple(maxima)

    ones_rows = 16
    value_rows = acc_ref.shape[2]

    def weigh(vt, p):
        lhs = jnp.concatenate([vt, jnp.ones((ones_rows, vt.shape[1]), BF16)], axis=0)
        r = _dot(lhs, p.astype(BF16))
        return r[:value_rows], r[value_rows:value_rows + 1]

    def absorb(qi, j, slot, maxima, first):
        for e in range(n):
            if first:
                m_new = maxima[e]
                p_top = jnp.exp2(s_buf[slot, e, 0:half, :] - m_new)
                p_bot = jnp.exp2(s_buf[slot, e, half:TILE, half:TILE] - upper(m_new))
                vt = vt_at[e](j)
                acc_top, l_top = weigh(lower(vt), p_top)
                acc_bot, l_bot = weigh(upper(vt), p_bot)
                l = join(lower(l_top), upper(l_top) + l_bot)
                acc = join(lower(acc_top), upper(acc_top) + acc_bot)
            else:
                m = m_ref[e, qi]
                m_new = jnp.maximum(m, maxima[e])
                alpha = jnp.exp2(m - m_new)
                pv, p_sum = weigh(vt_at[e](j), jnp.exp2(s_buf[slot, e] - m_new))
                l = alpha * l_ref[e, qi] + p_sum
                acc = alpha * acc_ref[e, qi] + pv
            m_ref[e, qi] = m_new
            l_ref[e, qi] = l
            acc_ref[e, qi] = acc

    def looped(t):
        return tbl_ref[2 * t], tbl_ref[2 * t + 1]

    pending = issue(*unrolled[0][:2], 0, unrolled[0][2])
    for idx, (qi, j, diagonal) in enumerate(unrolled):
        slot = idx % 2
        ahead = None
        if idx + 1 < len(unrolled):
            nxt = unrolled[idx + 1]
            ahead = issue(nxt[0], nxt[1], 1 - slot, nxt[2])
        elif n_looped:
            ahead = issue(*looped(0), 1 - slot, False)
        absorb(qi, j, slot, pending, diagonal)
        pending = ahead

    if n_looped:
        base = len(unrolled) % 2

        def body(trip, held):
            for k in range(TILES_PER_TRIP):
                t = TILES_PER_TRIP * trip + k
                slot = (base + k) % 2
                ahead = issue(*looped(t + 1), 1 - slot, False)
                absorb(*looped(t), slot, held, False)
                held = ahead
            return held

        lax.fori_loop(0, n_looped // TILES_PER_TRIP, body, pending)

    return lambda e, qi: acc_ref[e, qi] / l_ref[e, qi]


def _diff_attn_kernel(lambda_init, nq, tbl_ref, qt_ref, k_ref, vt_ref, lam_ref, subln_ref, o_ref, *scratch):
    feat = lax.broadcasted_iota(jnp.int32, (LANES, TILE), 0)
    zero = jnp.zeros((LANES, TILE), BF16)
    q1 = lambda qi: jnp.where(feat < DIFF_HEAD_DIM, qt_ref[0, 0, qi], zero)
    q2 = lambda qi: jnp.where(feat < DIFF_HEAD_DIM, zero, qt_ref[0, 0, qi])
    vt = lambda j: vt_ref[0, 0, j]
    out_t = _flash_head(tbl_ref, nq, [q1, q2], [k_ref, k_ref], [vt, vt], *scratch)
    lv = lam_ref[...]
    lam = (jnp.exp(jnp.sum(lv[0:1] * lv[1:2], axis=-1, keepdims=True))
           - jnp.exp(jnp.sum(lv[2:3] * lv[3:4], axis=-1, keepdims=True)) + lambda_init)
    for qi in range(nq):
        o = (out_t(0, qi) - lam * out_t(1, qi)).T
        o = o * lax.rsqrt(jnp.mean(o * o, axis=-1, keepdims=True) + EPS) * subln_ref[...]
        o_ref[qi * TILE:(qi + 1) * TILE, :] = (o * (1.0 - lambda_init)).astype(BF16)


def _mla_attn_kernel(nq, tbl_ref, qt_ref, k0_ref, k1_ref, vt_ref, o_ref, *scratch):
    out_t = _flash_head(tbl_ref, nq, [lambda qi: qt_ref[0, 0, qi], lambda qi: qt_ref[0, 1, qi]], [k0_ref, k1_ref],
                        [lambda j: vt_ref[0, 0, j], lambda j: vt_ref[0, 1, j]], *scratch)
    for qi in range(nq):
        for e in range(2):
            o_ref[qi * TILE:(qi + 1) * TILE, e * LANES:(e + 1) * LANES] = out_t(e, qi).T.astype(BF16)


def _fox_attn_kernel(nq, tbl_ref, qt_ref, k0_ref, k1_ref, vt_ref, og_ref, o_ref, *scratch):
    vt = [lambda j, e=e: vt_ref[0, 0, j, e * FOX_HEAD_DIM:(e + 1) * FOX_HEAD_DIM, :] for e in range(2)]
    out_t = _flash_head(tbl_ref, nq, [lambda qi: qt_ref[0, 0, qi], lambda qi: qt_ref[0, 1, qi]], [k0_ref, k1_ref],
                        vt, *scratch)
    for qi in range(nq):
        rows = slice(qi * TILE, (qi + 1) * TILE)
        o = jnp.concatenate([out_t(0, qi), out_t(1, qi)], axis=0).T
        o_ref[rows, :] = (o * _sigmoid(og_ref[rows, :])).astype(BF16)


def _head_tiles_spec(heads, seq):
    return pl.BlockSpec((1, heads, seq // TILE, LANES, TILE), lambda b, h, tbl: (b, h, 0, 0, 0))


def _head_rows_spec(seq, width, head_of=lambda h: h):
    return pl.BlockSpec((seq, width), lambda b, h, tbl: (b, head_of(h)))


def _attn_call(body, name, batch, groups, seq, in_specs, out_spec, out_shape, operands, value_rows=LANES):
    nq = seq // TILE
    table = jnp.asarray(_tile_order(nq)[2])
    stat = pltpu.VMEM((2, nq, 1, TILE), F32)
    return pl.pallas_call(
        functools.partial(body, nq),
        grid_spec=pltpu.PrefetchScalarGridSpec(
            num_scalar_prefetch=1,
            grid=(batch, groups),
            in_specs=in_specs,
            out_specs=out_spec,
            scratch_shapes=[pltpu.VMEM((2, 2, TILE, TILE), F32),
                            stat, stat, pltpu.VMEM((2, nq, value_rows, TILE), F32)],
        ),
        out_shape=out_shape,
        compiler_params=_params(("parallel", "parallel")),
        name=name,
    )(table, *operands)


def _diff_attn(qat, ka, vat, lam_rows, subln, lambda_init, batch, seq):
    const = lambda shape: pl.BlockSpec(shape, lambda b, h, tbl: (0, 0))
    return _attn_call(
        functools.partial(_diff_attn_kernel, lambda_init), "diff_attn", batch, DIFF_HEADS, seq,
        [_head_tiles_spec(1, seq), _head_rows_spec(seq, LANES), _head_tiles_spec(1, seq),
         const(lam_rows.shape), const(subln.shape)],
        _head_rows_spec(seq, LANES), jax.ShapeDtypeStruct(ka.shape, BF16), (qat, ka, vat, lam_rows, subln))


def _mla_attn(qbt, kb, vbt, batch, seq):
    return _attn_call(
        _mla_attn_kernel, "mla_attn", batch, MLA_HEADS // 2, seq,
        [_head_tiles_spec(2, seq), _head_rows_spec(seq, LANES, lambda p: 2 * p),
         _head_rows_spec(seq, LANES, lambda p: 2 * p + 1), _head_tiles_spec(2, seq)],
        _head_rows_spec(seq, 2 * LANES), jax.ShapeDtypeStruct(kb.shape, BF16), (qbt, kb, kb, vbt))


def _fox_attn(qft, kf, vft, og, batch, seq):
    return _attn_call(
        _fox_attn_kernel, "fox_attn", batch, FOX_HEADS // 2, seq,
        [_head_tiles_spec(2, seq), _head_rows_spec(seq, LANES, lambda p: 2 * p),
         _head_rows_spec(seq, LANES, lambda p: 2 * p + 1), _head_tiles_spec(1, seq), _head_rows_spec(seq, LANES)],
        _head_rows_spec(seq, LANES), jax.ShapeDtypeStruct(og.shape, BF16), (qft, kf, kf, vft, og),
        value_rows=FOX_HEAD_DIM)


def _out_proj_kernel(n_in, *refs):
    x_ref, g_ref = refs[0], refs[1]
    o_refs = refs[2:2 + n_in]
    w_refs = refs[2 + n_in:2 + 2 * n_in]
    out_ref = refs[2 + 2 * n_in]
    mix = _dot(o_refs[0][...], w_refs[0][...])
    for o_ref, w_ref in zip(o_refs[1:], w_refs[1:]):
        mix = mix + _dot(o_ref[...], w_ref[...])
    out_ref[...] = x_ref[...] + g_ref[0] * mix


def _out_proj(x, mod, os, ws, seq):
    t = x.shape[0]
    row = _row_spec(D_MODEL)
    return pl.pallas_call(
        functools.partial(_out_proj_kernel, len(os)),
        grid=(t // TILE,),
        in_specs=[row, _mod_spec(seq, 5)] + [_row_spec(o.shape[1]) for o in os] + [_resident(w.shape) for w in ws],
        out_specs=row,
        out_shape=jax.ShapeDtypeStruct(x.shape, F32),
        compiler_params=_params(("parallel",)),
        name="out_proj",
    )(x, mod, *os, *ws)


def _lane_rows(rows):
    width = max(r.shape[0] for r in rows)
    out = jnp.zeros((8, width), F32)
    for i, r in enumerate(rows):
        out = out.at[i, :r.shape[0]].set(r.astype(F32))
    return out


def _block_diag_ones(n, group):
    idx = np.arange(n) // group
    return jnp.asarray((idx[:, None] == idx[None, :]).astype(np.float32), BF16)


def _pad_heads(w, heads, dim):
    r = w.shape[0]
    w = w.reshape(r, heads, dim)
    return jnp.pad(w, ((0, 0), (0, 0), (0, LANES - dim))).reshape(r, heads * LANES)


def kernel(x, c, positions, ada_w, ada_b, ff1_gate, ff1_up, ff1_down, ff2_gate, ff2_up, ff2_down, ab_w_in, mla_w_qb, mla_w_kvb, mla_q_lat_g, mla_kv_lat_g, diff_q_g, diff_k_g, mla_q_g, mla_k_g, diff_lam_q1, diff_lam_k1, diff_lam_q2, diff_lam_k2, diff_subln_g, ab_w_out, fox_w_in, fox_b_f, fox_q_g, fox_k_g, fox_w_out):
    batch, seq, d = x.shape
    depth = ada_w.shape[0]
    t = batch * seq
    assert d == D_MODEL and seq % TILE == 0 and batch <= 8

    c_pad = jnp.zeros((8, d), F32).at[:batch].set(c)
    mod_all = _adaln(c_pad, ada_w, ada_b)

    inv_a = ROPE_THETA ** (-jnp.arange(0, DIFF_HEAD_DIM, 2, dtype=F32) / DIFF_HEAD_DIM)
    inv_b = ROPE_THETA ** (-jnp.arange(0, MLA_ROPE_DIM, 2, dtype=F32) / MLA_ROPE_DIM)
    inv_rows = jnp.zeros((8, LANES), F32).at[0, :32].set(inv_a).at[0, 64:80].set(inv_b)
    lane = np.arange(LANES)
    sgn_a = np.where(lane % 64 < 32, -1.0, 1.0)
    sgn_b = np.where((lane >= 64) & (lane < 80), -1.0, np.where((lane >= 80) & (lane < 96), 1.0, 0.0))
    sgn_rows = jnp.asarray(np.stack([sgn_a, sgn_b] + [np.zeros(LANES)] * 6), F32)
    pos_b = jnp.broadcast_to(positions.astype(F32).reshape(t, 1), (t, LANES))
    tables = _rope_tables(pos_b, inv_rows, sgn_rows)

    g64 = _block_diag_ones(SLAB, 64)
    g128 = _block_diag_ones(SLAB, LANES)
    bf = lambda w: w.astype(BF16)
    ff1 = [_to_bf16(w) for w in (ff1_gate, ff1_up, ff1_down)]
    ff2 = [_to_bf16(w) for w in (ff2_gate, ff2_up, ff2_down)]

    xt = x.reshape(t, d)
    for l in range(depth):
        mod = mod_all[l, :batch].reshape(batch * N_MOD, 1, d)
        xt = _ffn(xt, mod, 0, l, *ff1, seq)
        if l % 2 == 0:
            e = l // 2
            lambda_init = 0.8 - 0.6 * math.exp(-0.3 * l)
            w_in = ab_w_in[e]
            o_cq = 3 * DIFF_W
            o_kr = o_cq + MLA_Q_RANK + MLA_KV_RANK
            kr_pad = jnp.zeros((d, LANES), F32).at[:, MLA_NOPE_DIM:MLA_QK_DIM].set(w_in[:, o_kr:])
            w_all = bf(jnp.concatenate([w_in[:, :o_kr], kr_pad], axis=1))
            wqb = bf(_pad_heads(mla_w_qb[e], MLA_HEADS, MLA_QK_DIM))
            wkv = mla_w_kvb[e].reshape(MLA_KV_RANK, MLA_HEADS, MLA_NOPE_DIM + MLA_V_DIM)
            wkn = bf(_pad_heads(wkv[:, :, :MLA_NOPE_DIM].reshape(MLA_KV_RANK, -1), MLA_HEADS, MLA_NOPE_DIM))
            wvb = bf(wkv[:, :, MLA_NOPE_DIM:].reshape(MLA_KV_RANK, -1))
            pad128 = lambda g: jnp.pad(g, (0, LANES - g.shape[0]))
            gains = _lane_rows([jnp.tile(diff_q_g[e], 4), jnp.tile(diff_k_g[e], 4),
                                jnp.tile(pad128(mla_q_g[e]), 2), jnp.tile(pad128(mla_k_g[e]), 2)])
            qat, ka, vat, qbt, kb, vbt = _ab_proj(
                xt, mod, w_all, wqb, wkn, wvb, gains, mla_q_lat_g[e].reshape(1, -1), mla_kv_lat_g[e].reshape(1, -1),
                g64, g128, tables, batch, seq)
            lam_rows = _lane_rows([pad128(diff_lam_q1[e]), pad128(diff_lam_k1[e]),
                                   pad128(diff_lam_q2[e]), pad128(diff_lam_k2[e])])
            o_a = _diff_attn(qat, ka, vat, lam_rows, diff_subln_g[e].reshape(1, -1), lambda_init, batch, seq)
            o_b = _mla_attn(qbt, kb, vbt, batch, seq)
            w_out = bf(ab_w_out[e])
            xt = _out_proj(xt, mod, [o_a, o_b], [w_out[:DIFF_W], w_out[DIFF_W:]], seq)
        else:
            o = l // 2
            w_fox = bf(jnp.pad(fox_w_in[o], ((0, 0), (0, LANES - FOX_HEADS))))
            gains = _lane_rows([jnp.tile(fox_q_g[o], 4), jnp.tile(fox_k_g[o], 4)])
            bf_row = jnp.pad(fox_b_f[o], (0, LANES - FOX_HEADS)).reshape(1, LANES)
            tri = jnp.asarray(np.tril(np.ones((TILE, TILE), np.float32)), BF16)
            qft, kf, vft, og = _fox_proj(xt, mod, w_fox, gains, bf_row, g64, tri, batch, seq)
            o_f = _fox_attn(qft, kf, vft, og, batch, seq)
            xt = _out_proj(xt, mod, [o_f], [bf(fox_w_out[o])], seq)
        xt = _ffn(xt, mod, 6, l, *ff2, seq)
    return xt.reshape(batch, seq, d)
```

```python
import functools
import math

import numpy as np
import jax
import jax.numpy as jnp
from jax import lax
from jax.experimental import pallas as pl
from jax.experimental.pallas import tpu as pltpu

F32 = jnp.float32
BF16 = jnp.bfloat16

D_MODEL = 1024
D_FF = 2816
N_MOD = 9
ROPE_THETA = 10000.0
EPS = 1e-6
NEG_INF = -1e30
LOG2E = 1.4426950408889634

DIFF_HEADS = 8
DIFF_HEAD_DIM = 64
MLA_HEADS = 8
MLA_NOPE_DIM = 64
MLA_ROPE_DIM = 32
MLA_V_DIM = 128
MLA_Q_RANK = 384
MLA_KV_RANK = 256
FOX_HEADS = 16
FOX_HEAD_DIM = 64

DIFF_W = DIFF_HEADS * 2 * DIFF_HEAD_DIM
MLA_QK_DIM = MLA_NOPE_DIM + MLA_ROPE_DIM
FOX_W = FOX_HEADS * FOX_HEAD_DIM

LANES = 128
SLAB = 256
VMEM_LIMIT = 56 * 1024 * 1024

TILE = 512
FFN_TILE = 1024
FF_CHUNK = 256
TILES_PER_TRIP = 4

FOX_ONES_Q = (64, 65, 66)
FOX_BIAS_Q = (67, 68, 69)


def _params(sem):
    return pltpu.CompilerParams(dimension_semantics=sem, vmem_limit_bytes=VMEM_LIMIT)


def _dot(a, b):
    return jnp.dot(a, b, preferred_element_type=F32)


def _sigmoid(x):
    return 1.0 / (1.0 + jnp.exp(-x))


def _modulated(x, shift, scale):
    ms = jnp.mean(x * x, axis=-1, keepdims=True)
    return x * lax.rsqrt(ms + EPS) * (1.0 + scale) + shift


def _split3(x):
    hi = x.astype(BF16)
    r = x - hi.astype(F32)
    mid = r.astype(BF16)
    lo = (r - mid.astype(F32)).astype(BF16)
    return hi, mid, lo


def _group_sums(sq, gmat):
    return _dot(sq.astype(BF16), gmat)


def _rope(t, cos, sin_signed, half):
    n = t.shape[-1]
    lane = lax.broadcasted_iota(jnp.int32, t.shape, 1)
    up = pltpu.roll(t, n - half, 1)
    dn = pltpu.roll(t, half, 1)
    partner = jnp.where((lane & half) == 0, up, dn)
    return t * cos + partner * sin_signed


def _run_staged(stages):
    pending = stages[0][0]()
    for i, (_, consume) in enumerate(stages):
        ahead = stages[i + 1][0]() if i + 1 < len(stages) else None
        consume(pending)
        pending = ahead


def _put_transposed(ref, first_head, t):
    for j in range(t.shape[1] // LANES):
        ref[0, first_head + j, 0] = t[:, j * LANES:(j + 1) * LANES].T.astype(BF16)


def _cast_kernel(w_ref, o_ref):
    o_ref[...] = w_ref[...].astype(BF16)


def _to_bf16(w):
    depth, rows, cols = w.shape
    rb = rows // 2
    spec = pl.BlockSpec((1, rb, cols), lambda l, i: (l, i, 0))
    return pl.pallas_call(
        _cast_kernel,
        grid=(depth, rows // rb),
        in_specs=[spec],
        out_specs=spec,
        out_shape=jax.ShapeDtypeStruct(w.shape, BF16),
        compiler_params=_params(("parallel", "parallel")),
        name="to_bf16",
    )(w)


def _adaln_kernel(c_ref, w_ref, b_ref, o_ref):
    c = c_ref[...]
    cond = (c * _sigmoid(c)).astype(BF16)
    o_ref[0] = _dot(cond, w_ref[0].astype(BF16)) + b_ref[0]


def _adaln(c_pad, ada_w, ada_b):
    depth, d, n = ada_w.shape
    tn = 1024
    return pl.pallas_call(
        _adaln_kernel,
        grid=(depth, n // tn),
        in_specs=[
            pl.BlockSpec((8, d), lambda l, j: (0, 0)),
            pl.BlockSpec((1, d, tn), lambda l, j: (l, 0, j)),
            pl.BlockSpec((1, 1, tn), lambda l, j: (l, 0, j)),
        ],
        out_specs=pl.BlockSpec((1, 8, tn), lambda l, j: (l, 0, j)),
        out_shape=jax.ShapeDtypeStruct((depth, 8, n), F32),
        compiler_params=_params(("parallel", "parallel")),
        name="adaln",
    )(c_pad, ada_w, ada_b.reshape(depth, 1, n))


def _rope_table_kernel(pos_ref, inv_ref, sgn_ref, cos_a_ref, sin_a_ref, cos_b_ref, sin_b_ref):
    ang = pos_ref[...] * inv_ref[0:1, :]
    c = jnp.cos(ang)
    s = jnp.sin(ang)
    lane = lax.broadcasted_iota(jnp.int32, ang.shape, 1)
    in_a = lane < 32
    in_b = (lane >= 64) & (lane < 80)
    rope_b = (lane >= 64) & (lane < 96)

    def spread_a(t):
        t = jnp.where(in_a, t, 0.0)
        t = t + pltpu.roll(t, 32, 1)
        return t + pltpu.roll(t, 64, 1)

    def spread_b(t):
        t = jnp.where(in_b, t, 0.0)
        return t + pltpu.roll(t, 16, 1)

    cos_a_ref[...] = spread_a(c)
    sin_a_ref[...] = spread_a(s) * sgn_ref[0:1, :]
    cos_b_ref[...] = jnp.where(rope_b, spread_b(c), 1.0)
    sin_b_ref[...] = spread_b(s) * sgn_ref[1:2, :]


def _rope_tables(pos_b, inv_rows, sgn_rows):
    t = pos_b.shape[0]
    ts = TILE
    spec = pl.BlockSpec((ts, LANES), lambda i: (i, 0))
    cspec = pl.BlockSpec((8, LANES), lambda i: (0, 0))
    out = jax.ShapeDtypeStruct((t, LANES), F32)
    return pl.pallas_call(
        _rope_table_kernel,
        grid=(t // ts,),
        in_specs=[spec, cspec, cspec],
        out_specs=[spec, spec, spec, spec],
        out_shape=[out, out, out, out],
        compiler_params=_params(("parallel",)),
        name="rope_tables",
    )(pos_b, inv_rows, sgn_rows)


def _ffn_kernel(x_ref, sh_ref, sc_ref, g_ref, wg_ref, wu_ref, wd_ref, o_ref):
    x = x_ref[...]
    h = _modulated(x, sh_ref[0], sc_ref[0]).astype(BF16)
    acc = jnp.zeros(x.shape, F32)
    for c in range(D_FF // FF_CHUNK):
        cols = slice(c * FF_CHUNK, (c + 1) * FF_CHUNK)
        g = _dot(h, wg_ref[0, :, cols])
        u = _dot(h, wu_ref[0, :, cols])
        a = (g * _sigmoid(g) * u).astype(BF16)
        acc = acc + _dot(a, wd_ref[0, cols, :])
    o_ref[...] = x + (0.5 * g_ref[0]) * acc


def _mod_spec(seq, which):
    return pl.BlockSpec((1, 1, D_MODEL), lambda i: ((i * TILE // seq) * N_MOD + which, 0, 0))


def _resident(shape):
    return pl.BlockSpec(shape, lambda i: (0,) * len(shape), pipeline_mode=pl.Buffered(1))


def _row_spec(width):
    return pl.BlockSpec((TILE, width), lambda i: (i, 0))


def _ffn(x, mod, which, layer, wg, wu, wd, seq):
    t = x.shape[0]
    tm = FFN_TILE if seq % FFN_TILE == 0 else TILE
    row = pl.BlockSpec((tm, D_MODEL), lambda i: (i, 0))
    mod_row = lambda k: pl.BlockSpec((1, 1, D_MODEL), lambda i: ((i * tm // seq) * N_MOD + which + k, 0, 0))
    of_layer = lambda w: pl.BlockSpec((1,) + w.shape[1:], lambda i: (layer, 0, 0), pipeline_mode=pl.Buffered(1))
    return pl.pallas_call(
        _ffn_kernel,
        grid=(t // tm,),
        in_specs=[row, mod_row(0), mod_row(1), mod_row(2), of_layer(wg), of_layer(wu), of_layer(wd)],
        out_specs=row,
        out_shape=jax.ShapeDtypeStruct(x.shape, F32),
        compiler_params=_params(("parallel",)),
        name="ffn",
    )(x, mod, mod, mod, wg, wu, wd)


def _transposed_shape(batch, heads, seq):
    return jax.ShapeDtypeStruct((batch, heads, seq // TILE, LANES, TILE), BF16)


def _transposed_out_spec(heads, seq):
    chunks = seq // TILE
    return pl.BlockSpec((1, heads, 1, LANES, TILE), lambda i: (i // chunks, 0, i % chunks, 0, 0))


def _ab_proj_kernel(x_ref, sh_ref, sc_ref, w_ref, wqb_ref, wkn_ref, wvb_ref, gains_ref, lat_q_ref, lat_kv_ref,
                    g64_ref, g128_ref, cos_a_ref, sin_a_ref, cos_b_ref, sin_b_ref,
                    qat_ref, ka_ref, vat_ref, qbt_ref, kb_ref, vbt_ref):
    x = x_ref[...]
    h = _modulated(x, sh_ref[0], sc_ref[0]).astype(BF16)
    two = lambda r: jnp.concatenate([r, r], axis=1)
    cos_a, sin_a = two(cos_a_ref[...]), two(sin_a_ref[...])
    cos_b, sin_b = two(cos_b_ref[...]), two(sin_b_ref[...])
    gq_a, gk_a = gains_ref[0:1, :], gains_ref[1:2, :]
    gq_b, gk_b = gains_ref[2:3, :], gains_ref[3:4, :]
    g64, g128 = g64_ref[...], g128_ref[...]
    scale_a = DIFF_HEAD_DIM ** -0.5 * LOG2E
    scale_b = MLA_QK_DIM ** -0.5 * LOG2E
    heads_per_slab = SLAB // LANES
    o_cq = 3 * DIFF_W
    o_ckv = o_cq + MLA_Q_RANK
    o_kr = o_ckv + MLA_KV_RANK
    latent = {}

    def head_norm(t, gmat, dim, gain):
        ss = _group_sums(t * t, gmat)
        return t * lax.rsqrt(ss * (1.0 / dim) + EPS) * gain

    def project(cols):
        return lambda: _dot(h, w_ref[:, cols])

    def latent_inputs(y):
        c_q = y[:, :MLA_Q_RANK]
        c_q = c_q * lax.rsqrt(jnp.mean(c_q * c_q, axis=-1, keepdims=True) + EPS) * lat_q_ref[...]
        c_kv = y[:, MLA_Q_RANK:MLA_Q_RANK + MLA_KV_RANK]
        c_kv = c_kv * lax.rsqrt(jnp.mean(c_kv * c_kv, axis=-1, keepdims=True) + EPS) * lat_kv_ref[...]
        latent["q"], latent["kv"] = c_q.astype(BF16), c_kv.astype(BF16)
        latent["k_rope"] = two(y[:, MLA_Q_RANK + MLA_KV_RANK:])

    def diff_q(s):
        def consume(y):
            q = _rope(head_norm(y, g64, DIFF_HEAD_DIM, gq_a), cos_a, sin_a, 32)
            _put_transposed(qat_ref, heads_per_slab * s, q * scale_a)
        return consume

    def diff_k(s):
        def consume(y):
            k = _rope(head_norm(y, g64, DIFF_HEAD_DIM, gk_a), cos_a, sin_a, 32)
            ka_ref[:, s * SLAB:(s + 1) * SLAB] = k.astype(BF16)
        return consume

    def mla_q(s):
        def consume(y):
            q = _rope(head_norm(y, g128, MLA_QK_DIM, gq_b), cos_b, sin_b, 16)
            _put_transposed(qbt_ref, heads_per_slab * s, q * scale_b)
        return consume

    def mla_k(s):
        def consume(y):
            k = _rope(head_norm(y + latent["k_rope"], g128, MLA_QK_DIM, gk_b), cos_b, sin_b, 16)
            kb_ref[:, s * SLAB:(s + 1) * SLAB] = k.astype(BF16)
        return consume

    slab = lambda base, s: slice(base + s * SLAB, base + (s + 1) * SLAB)
    n_slabs = DIFF_W // SLAB
    stages = [(project(slice(o_cq, o_kr + LANES)), latent_inputs)]
    for s in range(n_slabs):
        stages.append((project(slab(0, s)), diff_q(s)))
        stages.append((project(slab(DIFF_W, s)), diff_k(s)))
        stages.append((project(slab(2 * DIFF_W, s)),
                       lambda y, s=s: _put_transposed(vat_ref, heads_per_slab * s, y)))
    for s in range(n_slabs):
        cols = slab(0, s)
        stages.append((lambda cols=cols: _dot(latent["q"], wqb_ref[:, cols]), mla_q(s)))
        stages.append((lambda cols=cols: _dot(latent["kv"], wkn_ref[:, cols]), mla_k(s)))
        stages.append((lambda cols=cols: _dot(latent["kv"], wvb_ref[:, cols]),
                       lambda y, s=s: _put_transposed(vbt_ref, heads_per_slab * s, y)))
    _run_staged(stages)


def _ab_proj(x, mod, w_all, wqb, wkn, wvb, gains, lat_q, lat_kv, g64, g128, tables, batch, seq):
    t = x.shape[0]
    row = _row_spec(D_MODEL)
    tab = _row_spec(LANES)
    rows_out = jax.ShapeDtypeStruct((t, D_MODEL), BF16)
    tr_out = _transposed_shape(batch, DIFF_HEADS, seq)
    tr_spec = _transposed_out_spec(DIFF_HEADS, seq)
    return pl.pallas_call(
        _ab_proj_kernel,
        grid=(t // TILE,),
        in_specs=[row, _mod_spec(seq, 3), _mod_spec(seq, 4),
                  _resident(w_all.shape), _resident(wqb.shape), _resident(wkn.shape), _resident(wvb.shape),
                  _resident(gains.shape), _resident(lat_q.shape), _resident(lat_kv.shape),
                  _resident(g64.shape), _resident(g128.shape), tab, tab, tab, tab],
        out_specs=[tr_spec, row, tr_spec, tr_spec, row, tr_spec],
        out_shape=[tr_out, rows_out, tr_out, tr_out, rows_out, tr_out],
        compiler_params=_params(("parallel",)),
        name="ab_proj",
    )(x, mod, mod, w_all, wqb, wkn, wvb, gains, lat_q, lat_kv, g64, g128, *tables)


def _fox_proj_kernel(seq, x_ref, sh_ref, sc_ref, w_ref, gains_ref, bf_ref, g64_ref, tri_ref,
                     qft_ref, kf_ref, vft_ref, og_ref, carry_ref):
    i = pl.program_id(0)
    tm = x_ref.shape[0]

    @pl.when((i * tm) % seq == 0)
    def _():
        carry_ref[...] = jnp.zeros(carry_ref.shape, F32)

    x = x_ref[...]
    h = _modulated(x, sh_ref[0], sc_ref[0]).astype(BF16)
    lane = lax.broadcasted_iota(jnp.int32, (tm, LANES), 1)
    gq, gk = gains_ref[0:1, :], gains_ref[1:2, :]
    g64 = g64_ref[...]
    scale = FOX_HEAD_DIM ** -0.5 * LOG2E
    heads_per_slab = SLAB // FOX_HEAD_DIM
    gate = {}

    def project(cols):
        return lambda: _dot(h, w_ref[:, cols])

    def forget_cumsum(f):
        z = f + bf_ref[...]
        log_f = jnp.where(lane < FOX_HEADS, jnp.minimum(z, 0.0) - jnp.log1p(jnp.exp(-jnp.abs(z))), 0.0)
        tri = tri_ref[...]
        hi, mid, lo = _split3(log_f)
        cum = _dot(tri, hi) + _dot(tri, mid) + _dot(tri, lo) + carry_ref[0:1, :]
        carry_ref[0:1, :] = cum[tm - 1:tm, :]
        gate["cum"] = cum * LOG2E

    def head_norm(t, gain):
        ss = _group_sums(t * t, g64)
        return t * lax.rsqrt(ss * (1.0 / FOX_HEAD_DIM) + EPS) * gain

    def bias_lanes(pieces, ones_at, pieces_at, sign):
        blk = jnp.zeros((tm, LANES), F32)
        for ln in ones_at:
            blk = jnp.where(lane == ln, 1.0, blk)
        for ln, p in zip(pieces_at, pieces):
            blk = jnp.where(lane == ln, sign * p.astype(F32), blk)
        return blk

    def head_blocks(t, s, is_q):
        for j in range(heads_per_slab):
            hd = s * heads_per_slab + j
            th = t[:, (j // 2) * LANES:(j // 2 + 1) * LANES]
            if j % 2:
                th = pltpu.roll(th, FOX_HEAD_DIM, 1)
            pieces = _split3(jnp.broadcast_to(gate["cum"][:, hd:hd + 1], (tm, LANES)))
            if is_q:
                aug = bias_lanes(pieces, FOX_ONES_Q, FOX_BIAS_Q, 1.0)
                _put_transposed(qft_ref, hd, jnp.where(lane < FOX_HEAD_DIM, th, aug))
            else:
                aug = bias_lanes(pieces, FOX_BIAS_Q, FOX_ONES_Q, -1.0)
                kf_ref[:, hd * LANES:(hd + 1) * LANES] = jnp.where(lane < FOX_HEAD_DIM, th, aug).astype(BF16)

    def store_gate(s):
        def consume(y):
            og_ref[:, s * SLAB:(s + 1) * SLAB] = y
        return consume

    slab = lambda base, s: slice(base + s * SLAB, base + (s + 1) * SLAB)
    stages = [(project(slice(4 * FOX_W, 4 * FOX_W + LANES)), forget_cumsum)]
    for s in range(FOX_W // SLAB):
        stages.append((project(slab(2 * FOX_W, s)),
                       lambda y, s=s: _put_transposed(vft_ref, (SLAB // LANES) * s, y)))
        stages.append((project(slab(3 * FOX_W, s)), store_gate(s)))
    for s in range(FOX_W // SLAB):
        stages.append((project(slab(0, s)), lambda y, s=s: head_blocks(head_norm(y, gq) * scale, s, True)))
        stages.append((project(slab(FOX_W, s)), lambda y, s=s: head_blocks(head_norm(y, gk), s, False)))
    _run_staged(stages)


def _fox_proj(x, mod, w_fox, gains, bf_row, g64, tri, batch, seq):
    t = x.shape[0]
    row = _row_spec(D_MODEL)
    return pl.pallas_call(
        functools.partial(_fox_proj_kernel, seq),
        grid=(t // TILE,),
        in_specs=[row, _mod_spec(seq, 3), _mod_spec(seq, 4), _resident(w_fox.shape),
                  _resident(gains.shape), _resident(bf_row.shape), _resident(g64.shape), _resident(tri.shape)],
        out_specs=[_transposed_out_spec(FOX_HEADS, seq), _row_spec(FOX_HEADS * LANES),
                   _transposed_out_spec(FOX_HEADS // 2, seq), row],
        out_shape=[_transposed_shape(batch, FOX_HEADS, seq), jax.ShapeDtypeStruct((t, FOX_HEADS * LANES), BF16),
                   _transposed_shape(batch, FOX_HEADS // 2, seq), jax.ShapeDtypeStruct((t, FOX_W), F32)],
        scratch_shapes=[pltpu.VMEM((8, LANES), F32)],
        compiler_params=_params(("arbitrary",)),
        name="fox_proj",
    )(x, mod, mod, w_fox, gains, bf_row, g64, tri)


def _tile_order(nq):
    below = [(qi, j) for qi in range(1, nq) for j in range(qi)]
    peeled = len(below) % TILES_PER_TRIP
    unrolled = [(i, i, True) for i in range(nq)] + [(qi, j, False) for qi, j in below[:peeled]]
    looped = below[peeled:]
    table = np.asarray(looped + looped[-1:], np.int32).reshape(-1) if looped else np.zeros((2,), np.int32)
    return unrolled, len(looped), table


def _flash_head(tbl_ref, nq, q_at, k_refs, vt_at, s_buf, m_ref, l_ref, acc_ref):
    n = len(k_refs)
    unrolled, n_looped, _ = _tile_order(nq)

    half = TILE // 2
    lower = lambda t: t[:, :half]
    upper = lambda t: t[:, half:]
    join = lambda a, b: jnp.concatenate([a, b], axis=1)

    def issue_diagonal(qi, slot):
        off = qi * TILE
        key = lax.broadcasted_iota(jnp.int32, (half, half), 0)
        query = lax.broadcasted_iota(jnp.int32, (half, half), 1)
        causal = key <= query
        maxima = []
        for e in range(n):
            qt = q_at[e](qi)
            top = _dot(k_refs[e][off:off + half, :], qt)
            top = join(jnp.where(causal, lower(top), NEG_INF), upper(top))
            bot = jnp.where(causal, _dot(k_refs[e][off + half:off + TILE, :], upper(qt)), NEG_INF)
            s_buf[slot, e, 0:half, :] = top
            s_buf[slot, e, half:TILE, half:TILE] = bot
            top_max = jnp.max(top, axis=0, keepdims=True)
            maxima.append(join(lower(top_max), jnp.maximum(upper(top_max), jnp.max(bot, axis=0, keepdims=True))))
        return tuple(maxima)

    def issue(qi, j, slot, diagonal):
        if diagonal:
            return issue_diagonal(qi, slot)
        off = pl.multiple_of(j * TILE, TILE)
        maxima = []
        for e in range(n):
            s = _dot(k_refs[e][pl.ds(off, TILE), :], q_at[e](qi))
            s_buf[slot, e] = s
            maxima.append(jnp.max(s, axis=0, keepdims=True))
        return tuple(maxima)

    ones_rows = 16
    value_rows = acc_ref.shape[2]

    def weigh(vt, p):
        lhs = jnp.concatenate([vt, jnp.ones((ones_rows, vt.shape[1]), BF16)], axis=0)
        r = _dot(lhs, p.astype(BF16))
        return r[:value_rows], r[value_rows:value_rows + 1]

    def absorb(qi, j, slot, maxima, first):
        for e in range(n):
            if first:
                m_new = maxima[e]
                p_top = jnp.exp2(s_buf[slot, e, 0:half, :] - m_new)
                p_bot = jnp.exp2(s_buf[slot, e, half:TILE, half:TILE] - upper(m_new))
                vt = vt_at[e](j)
                acc_top, l_top = weigh(lower(vt), p_top)
                acc_bot, l_bot = weigh(upper(vt), p_bot)
                l = join(lower(l_top), upper(l_top) + l_bot)
                acc = join(lower(acc_top), upper(acc_top) + acc_bot)
            else:
                m = m_ref[e, qi]
                m_new = jnp.maximum(m, maxima[e])
                alpha = jnp.exp2(m - m_new)
                pv, p_sum = weigh(vt_at[e](j), jnp.exp2(s_buf[slot, e] - m_new))
                l = alpha * l_ref[e, qi] + p_sum
                acc = alpha * acc_ref[e, qi] + pv
            m_ref[e, qi] = m_new
            l_ref[e, qi] = l
            acc_ref[e, qi] = acc

    def looped(t):
        return tbl_ref[2 * t], tbl_ref[2 * t + 1]

    pending = issue(*unrolled[0][:2], 0, unrolled[0][2])
    for idx, (qi, j, diagonal) in enumerate(unrolled):
        slot = idx % 2
        ahead = None
        if idx + 1 < len(unrolled):
            nxt = unrolled[idx + 1]
            ahead = issue(nxt[0], nxt[1], 1 - slot, nxt[2])
        elif n_looped:
            ahead = issue(*looped(0), 1 - slot, False)
        absorb(qi, j, slot, pending, diagonal)
        pending = ahead

    if n_looped:
        base = len(unrolled) % 2

        def body(trip, held):
            for k in range(TILES_PER_TRIP):
                t = TILES_PER_TRIP * trip + k
                slot = (base + k) % 2
                ahead = issue(*looped(t + 1), 1 - slot, False)
                absorb(*looped(t), slot, held, False)
                held = ahead
            return held

        lax.fori_loop(0, n_looped // TILES_PER_TRIP, body, pending)

    return lambda e, qi: acc_ref[e, qi] / l_ref[e, qi]


def _diff_attn_kernel(lambda_init, nq, tbl_ref, qt_ref, k_ref, vt_ref, lam_ref, subln_ref, o_ref, *scratch):
    feat = lax.broadcasted_iota(jnp.int32, (LANES, TILE), 0)
    zero = jnp.zeros((LANES, TILE), BF16)
    q1 = lambda qi: jnp.where(feat < DIFF_HEAD_DIM, qt_ref[0, 0, qi], zero)
    q2 = lambda qi: jnp.where(feat < DIFF_HEAD_DIM, zero, qt_ref[0, 0, qi])
    vt = lambda j: vt_ref[0, 0, j]
    out_t = _flash_head(tbl_ref, nq, [q1, q2], [k_ref, k_ref], [vt, vt], *scratch)
    lv = lam_ref[...]
    lam = (jnp.exp(jnp.sum(lv[0:1] * lv[1:2], axis=-1, keepdims=True))
           - jnp.exp(jnp.sum(lv[2:3] * lv[3:4], axis=-1, keepdims=True)) + lambda_init)
    for qi in range(nq):
        o = (out_t(0, qi) - lam * out_t(1, qi)).T
        o = o * lax.rsqrt(jnp.mean(o * o, axis=-1, keepdims=True) + EPS) * subln_ref[...]
        o_ref[qi * TILE:(qi + 1) * TILE, :] = (o * (1.0 - lambda_init)).astype(BF16)


def _mla_attn_kernel(nq, tbl_ref, qt_ref, k0_ref, k1_ref, vt_ref, o_ref, *scratch):
    out_t = _flash_head(tbl_ref, nq, [lambda qi: qt_ref[0, 0, qi], lambda qi: qt_ref[0, 1, qi]], [k0_ref, k1_ref],
                        [lambda j: vt_ref[0, 0, j], lambda j: vt_ref[0, 1, j]], *scratch)
    for qi in range(nq):
        for e in range(2):
            o_ref[qi * TILE:(qi + 1) * TILE, e * LANES:(e + 1) * LANES] = out_t(e, qi).T.astype(BF16)


def _fox_attn_kernel(nq, tbl_ref, qt_ref, k0_ref, k1_ref, vt_ref, og_ref, o_ref, *scratch):
    vt = [lambda j, e=e: vt_ref[0, 0, j, e * FOX_HEAD_DIM:(e + 1) * FOX_HEAD_DIM, :] for e in range(2)]
    out_t = _flash_head(tbl_ref, nq, [lambda qi: qt_ref[0, 0, qi], lambda qi: qt_ref[0, 1, qi]], [k0_ref, k1_ref],
                        vt, *scratch)
    for qi in range(nq):
        rows = slice(qi * TILE, (qi + 1) * TILE)
        o = jnp.concatenate([out_t(0, qi), out_t(1, qi)], axis=0).T
        o_ref[rows, :] = (o * _sigmoid(og_ref[rows, :])).astype(BF16)


def _head_tiles_spec(heads, seq):
    return pl.BlockSpec((1, heads, seq // TILE, LANES, TILE), lambda b, h, tbl: (b, h, 0, 0, 0))


def _head_rows_spec(seq, width, head_of=lambda h: h):
    return pl.BlockSpec((seq, width), lambda b, h, tbl: (b, head_of(h)))


def _attn_call(body, name, batch, groups, seq, in_specs, out_spec, out_shape, operands, value_rows=LANES):
    nq = seq // TILE
    table = jnp.asarray(_tile_order(nq)[2])
    stat = pltpu.VMEM((2, nq, 1, TILE), F32)
    return pl.pallas_call(
        functools.partial(body, nq),
        grid_spec=pltpu.PrefetchScalarGridSpec(
            num_scalar_prefetch=1,
            grid=(batch, groups),
            in_specs=in_specs,
            out_specs=out_spec,
            scratch_shapes=[pltpu.VMEM((2, 2, TILE, TILE), F32),
                            stat, stat, pltpu.VMEM((2, nq, value_rows, TILE), F32)],
        ),
        out_shape=out_shape,
        compiler_params=_params(("parallel", "parallel")),
        name=name,
    )(table, *operands)


def _diff_attn(qat, ka, vat, lam_rows, subln, lambda_init, batch, seq):
    const = lambda shape: pl.BlockSpec(shape, lambda b, h, tbl: (0, 0))
    return _attn_call(
        functools.partial(_diff_attn_kernel, lambda_init), "diff_attn", batch, DIFF_HEADS, seq,
        [_head_tiles_spec(1, seq), _head_rows_spec(seq, LANES), _head_tiles_spec(1, seq),
         const(lam_rows.shape), const(subln.shape)],
        _head_rows_spec(seq, LANES), jax.ShapeDtypeStruct(ka.shape, BF16), (qat, ka, vat, lam_rows, subln))


def _mla_attn(qbt, kb, vbt, batch, seq):
    return _attn_call(
        _mla_attn_kernel, "mla_attn", batch, MLA_HEADS // 2, seq,
        [_head_tiles_spec(2, seq), _head_rows_spec(seq, LANES, lambda p: 2 * p),
         _head_rows_spec(seq, LANES, lambda p: 2 * p + 1), _head_tiles_spec(2, seq)],
        _head_rows_spec(seq, 2 * LANES), jax.ShapeDtypeStruct(kb.shape, BF16), (qbt, kb, kb, vbt))


def _fox_attn(qft, kf, vft, og, batch, seq):
    return _attn_call(
        _fox_attn_kernel, "fox_attn", batch, FOX_HEADS // 2, seq,
        [_head_tiles_spec(2, seq), _head_rows_spec(seq, LANES, lambda p: 2 * p),
         _head_rows_spec(seq, LANES, lambda p: 2 * p + 1), _head_tiles_spec(1, seq), _head_rows_spec(seq, LANES)],
        _head_rows_spec(seq, LANES), jax.ShapeDtypeStruct(og.shape, BF16), (qft, kf, kf, vft, og),
        value_rows=FOX_HEAD_DIM)


def _out_proj_kernel(n_in, *refs):
    x_ref, g_ref = refs[0], refs[1]
    o_refs = refs[2:2 + n_in]
    w_refs = refs[2 + n_in:2 + 2 * n_in]
    out_ref = refs[2 + 2 * n_in]
    mix = _dot(o_refs[0][...], w_refs[0][...])
    for o_ref, w_ref in zip(o_refs[1:], w_refs[1:]):
        mix = mix + _dot(o_ref[...], w_ref[...])
    out_ref[...] = x_ref[...] + g_ref[0] * mix


def _out_proj(x, mod, os, ws, seq):
    t = x.shape[0]
    row = _row_spec(D_MODEL)
    return pl.pallas_call(
        functools.partial(_out_proj_kernel, len(os)),
        grid=(t // TILE,),
        in_specs=[row, _mod_spec(seq, 5)] + [_row_spec(o.shape[1]) for o in os] + [_resident(w.shape) for w in ws],
        out_specs=row,
        out_shape=jax.ShapeDtypeStruct(x.shape, F32),
        compiler_params=_params(("parallel",)),
        name="out_proj",
    )(x, mod, *os, *ws)


def _lane_rows(rows):
    width = max(r.shape[0] for r in rows)
    out = jnp.zeros((8, width), F32)
    for i, r in enumerate(rows):
        out = out.at[i, :r.shape[0]].set(r.astype(F32))
    return out


def _block_diag_ones(n, group):
    idx = np.arange(n) // group
    return jnp.asarray((idx[:, None] == idx[None, :]).astype(np.float32), BF16)


def _pad_heads(w, heads, dim):
    r = w.shape[0]
    w = w.reshape(r, heads, dim)
    return jnp.pad(w, ((0, 0), (0, 0), (0, LANES - dim))).reshape(r, heads * LANES)


def kernel(x, c, positions, ada_w, ada_b, ff1_gate, ff1_up, ff1_down, ff2_gate, ff2_up, ff2_down, ab_w_in, mla_w_qb, mla_w_kvb, mla_q_lat_g, mla_kv_lat_g, diff_q_g, diff_k_g, mla_q_g, mla_k_g, diff_lam_q1, diff_lam_k1, diff_lam_q2, diff_lam_k2, diff_subln_g, ab_w_out, fox_w_in, fox_b_f, fox_q_g, fox_k_g, fox_w_out):
    batch, seq, d = x.shape
    depth = ada_w.shape[0]
    t = batch * seq
    assert d == D_MODEL and seq % TILE == 0 and batch <= 8

    c_pad = jnp.zeros((8, d), F32).at[:batch].set(c)
    mod_all = _adaln(c_pad, ada_w, ada_b)

    inv_a = ROPE_THETA ** (-jnp.arange(0, DIFF_HEAD_DIM, 2, dtype=F32) / DIFF_HEAD_DIM)
    inv_b = ROPE_THETA ** (-jnp.arange(0, MLA_ROPE_DIM, 2, dtype=F32) / MLA_ROPE_DIM)
    inv_rows = jnp.zeros((8, LANES), F32).at[0, :32].set(inv_a).at[0, 64:80].set(inv_b)
    lane = np.arange(LANES)
    sgn_a = np.where(lane % 64 < 32, -1.0, 1.0)
    sgn_b = np.where((lane >= 64) & (lane < 80), -1.0, np.where((lane >= 80) & (lane < 96), 1.0, 0.0))
    sgn_rows = jnp.asarray(np.stack([sgn_a, sgn_b] + [np.zeros(LANES)] * 6), F32)
    pos_b = jnp.broadcast_to(positions.astype(F32).reshape(t, 1), (t, LANES))
    tables = _rope_tables(pos_b, inv_rows, sgn_rows)

    g64 = _block_diag_ones(SLAB, 64)
    g128 = _block_diag_ones(SLAB, LANES)
    bf = lambda w: w.astype(BF16)
    ff1 = [_to_bf16(w) for w in (ff1_gate, ff1_up, ff1_down)]
    ff2 = [_to_bf16(w) for w in (ff2_gate, ff2_up, ff2_down)]

    xt = x.reshape(t, d)
    for l in range(depth):
        mod = mod_all[l, :batch].reshape(batch * N_MOD, 1, d)
        xt = _ffn(xt, mod, 0, l, *ff1, seq)
        if l % 2 == 0:
            e = l // 2
            lambda_init = 0.8 - 0.6 * math.exp(-0.3 * l)
            w_in = ab_w_in[e]
            o_cq = 3 * DIFF_W
            o_kr = o_cq + MLA_Q_RANK + MLA_KV_RANK
            kr_pad = jnp.zeros((d, LANES), F32).at[:, MLA_NOPE_DIM:MLA_QK_DIM].set(w_in[:, o_kr:])
            w_all = bf(jnp.concatenate([w_in[:, :o_kr], kr_pad], axis=1))
            wqb = bf(_pad_heads(mla_w_qb[e], MLA_HEADS, MLA_QK_DIM))
            wkv = mla_w_kvb[e].reshape(MLA_KV_RANK, MLA_HEADS, MLA_NOPE_DIM + MLA_V_DIM)
            wkn = bf(_pad_heads(wkv[:, :, :MLA_NOPE_DIM].reshape(MLA_KV_RANK, -1), MLA_HEADS, MLA_NOPE_DIM))
            wvb = bf(wkv[:, :, MLA_NOPE_DIM:].reshape(MLA_KV_RANK, -1))
            pad128 = lambda g: jnp.pad(g, (0, LANES - g.shape[0]))
            gains = _lane_rows([jnp.tile(diff_q_g[e], 4), jnp.tile(diff_k_g[e], 4),
                                jnp.tile(pad128(mla_q_g[e]), 2), jnp.tile(pad128(mla_k_g[e]), 2)])
            qat, ka, vat, qbt, kb, vbt = _ab_proj(
                xt, mod, w_all, wqb, wkn, wvb, gains, mla_q_lat_g[e].reshape(1, -1), mla_kv_lat_g[e].reshape(1, -1),
                g64, g128, tables, batch, seq)
            lam_rows = _lane_rows([pad128(diff_lam_q1[e]), pad128(diff_lam_k1[e]),
                                   pad128(diff_lam_q2[e]), pad128(diff_lam_k2[e])])
            o_a = _diff_attn(qat, ka, vat, lam_rows, diff_subln_g[e].reshape(1, -1), lambda_init, batch, seq)
            o_b = _mla_attn(qbt, kb, vbt, batch, seq)
            w_out = bf(ab_w_out[e])
            xt = _out_proj(xt, mod, [o_a, o_b], [w_out[:DIFF_W], w_out[DIFF_W:]], seq)
        else:
            o = l // 2
            w_fox = bf(jnp.pad(fox_w_in[o], ((0, 0), (0, LANES - FOX_HEADS))))
            gains = _lane_rows([jnp.tile(fox_q_g[o], 4), jnp.tile(fox_k_g[o], 4)])
            bf_row = jnp.pad(fox_b_f[o], (0, LANES - FOX_HEADS)).reshape(1, LANES)
            tri = jnp.asarray(np.tril(np.ones((TILE, TILE), np.float32)), BF16)
            qft, kf, vft, og = _fox_proj(xt, mod, w_fox, gains, bf_row, g64, tri, batch, seq)
            o_f = _fox_attn(qft, kf, vft, og, batch, seq)
            xt = _out_proj(xt, mod, [o_f], [bf(fox_w_out[o])], seq)
        xt = _ffn(xt, mod, 6, l, *ff2, seq)
    return xt.reshape(batch, seq, d)
```

```python
import functools
import math

import numpy as np
import jax
import jax.numpy as jnp
from jax import lax
from jax.experimental import pallas as pl
from jax.experimental.pallas import tpu as pltpu

F32 = jnp.float32
BF16 = jnp.bfloat16

D_MODEL = 1024
D_FF = 2816
N_MOD = 9
ROPE_THETA = 10000.0
EPS = 1e-6
NEG_INF = -1e30
LOG2E = 1.4426950408889634

DIFF_HEADS = 8
DIFF_HEAD_DIM = 64
MLA_HEADS = 8
MLA_NOPE_DIM = 64
MLA_ROPE_DIM = 32
MLA_V_DIM = 128
MLA_Q_RANK = 384
MLA_KV_RANK = 256
FOX_HEADS = 16
FOX_HEAD_DIM = 64

DIFF_W = DIFF_HEADS * 2 * DIFF_HEAD_DIM
MLA_QK_DIM = MLA_NOPE_DIM + MLA_ROPE_DIM
FOX_W = FOX_HEADS * FOX_HEAD_DIM

LANES = 128
SLAB = 256
VMEM_LIMIT = 56 * 1024 * 1024

TILE = 512
FFN_TILE = 1024
FF_CHUNK = 256
TILES_PER_TRIP = 14

FOX_ONES_Q = (64, 65, 66)
FOX_BIAS_Q = (67, 68, 69)


def _params(sem):
    return pltpu.CompilerParams(dimension_semantics=sem, vmem_limit_bytes=VMEM_LIMIT)


def _dot(a, b):
    return jnp.dot(a, b, preferred_element_type=F32)


def _sigmoid(x):
    return 1.0 / (1.0 + jnp.exp(-x))


def _modulated(x, shift, scale):
    ms = jnp.mean(x * x, axis=-1, keepdims=True)
    return x * lax.rsqrt(ms + EPS) * (1.0 + scale) + shift


def _split3(x):
    hi = x.astype(BF16)
    r = x - hi.astype(F32)
    mid = r.astype(BF16)
    lo = (r - mid.astype(F32)).astype(BF16)
    return hi, mid, lo


def _group_sums(sq, gmat):
    return _dot(sq.astype(BF16), gmat)


def _rope(t, cos, sin_signed, half):
    n = t.shape[-1]
    lane = lax.broadcasted_iota(jnp.int32, t.shape, 1)
    up = pltpu.roll(t, n - half, 1)
    dn = pltpu.roll(t, half, 1)
    partner = jnp.where((lane & half) == 0, up, dn)
    return t * cos + partner * sin_signed


def _run_staged(stages):
    pending = stages[0][0]()
    for i, (_, consume) in enumerate(stages):
        ahead = stages[i + 1][0]() if i + 1 < len(stages) else None
        consume(pending)
        pending = ahead


def _put_transposed(ref, first_head, t):
    for j in range(t.shape[1] // LANES):
        ref[0, first_head + j, 0] = t[:, j * LANES:(j + 1) * LANES].T.astype(BF16)


def _cast_kernel(w_ref, o_ref):
    o_ref[...] = w_ref[...].astype(BF16)


def _to_bf16(w):
    depth, rows, cols = w.shape
    rb = rows // 2
    spec = pl.BlockSpec((1, rb, cols), lambda l, i: (l, i, 0))
    return pl.pallas_call(
        _cast_kernel,
        grid=(depth, rows // rb),
        in_specs=[spec],
        out_specs=spec,
        out_shape=jax.ShapeDtypeStruct(w.shape, BF16),
        compiler_params=_params(("parallel", "parallel")),
        name="to_bf16",
    )(w)


def _adaln_kernel(c_ref, w_ref, b_ref, o_ref):
    c = c_ref[...]
    cond = (c * _sigmoid(c)).astype(BF16)
    o_ref[0] = _dot(cond, w_ref[0].astype(BF16)) + b_ref[0]


def _adaln(c_pad, ada_w, ada_b):
    depth, d, n = ada_w.shape
    tn = 1024
    return pl.pallas_call(
        _adaln_kernel,
        grid=(depth, n // tn),
        in_specs=[
            pl.BlockSpec((8, d), lambda l, j: (0, 0)),
            pl.BlockSpec((1, d, tn), lambda l, j: (l, 0, j)),
            pl.BlockSpec((1, 1, tn), lambda l, j: (l, 0, j)),
        ],
        out_specs=pl.BlockSpec((1, 8, tn), lambda l, j: (l, 0, j)),
        out_shape=jax.ShapeDtypeStruct((depth, 8, n), F32),
        compiler_params=_params(("parallel", "parallel")),
        name="adaln",
    )(c_pad, ada_w, ada_b.reshape(depth, 1, n))


def _rope_table_kernel(pos_ref, inv_ref, sgn_ref, cos_a_ref, sin_a_ref, cos_b_ref, sin_b_ref):
    ang = pos_ref[...] * inv_ref[0:1, :]
    c = jnp.cos(ang)
    s = jnp.sin(ang)
    lane = lax.broadcasted_iota(jnp.int32, ang.shape, 1)
    in_a = lane < 32
    in_b = (lane >= 64) & (lane < 80)
    rope_b = (lane >= 64) & (lane < 96)

    def spread_a(t):
        t = jnp.where(in_a, t, 0.0)
        t = t + pltpu.roll(t, 32, 1)
        return t + pltpu.roll(t, 64, 1)

    def spread_b(t):
        t = jnp.where(in_b, t, 0.0)
        return t + pltpu.roll(t, 16, 1)

    cos_a_ref[...] = spread_a(c)
    sin_a_ref[...] = spread_a(s) * sgn_ref[0:1, :]
    cos_b_ref[...] = jnp.where(rope_b, spread_b(c), 1.0)
    sin_b_ref[...] = spread_b(s) * sgn_ref[1:2, :]


def _rope_tables(pos_b, inv_rows, sgn_rows):
    t = pos_b.shape[0]
    ts = TILE
    spec = pl.BlockSpec((ts, LANES), lambda i: (i, 0))
    cspec = pl.BlockSpec((8, LANES), lambda i: (0, 0))
    out = jax.ShapeDtypeStruct((t, LANES), F32)
    return pl.pallas_call(
        _rope_table_kernel,
        grid=(t // ts,),
        in_specs=[spec, cspec, cspec],
        out_specs=[spec, spec, spec, spec],
        out_shape=[out, out, out, out],
        compiler_params=_params(("parallel",)),
        name="rope_tables",
    )(pos_b, inv_rows, sgn_rows)


def _ffn_kernel(x_ref, sh_ref, sc_ref, g_ref, wg_ref, wu_ref, wd_ref, o_ref):
    x = x_ref[...]
    h = _modulated(x, sh_ref[0], sc_ref[0]).astype(BF16)
    acc = jnp.zeros(x.shape, F32)
    for c in range(D_FF // FF_CHUNK):
        cols = slice(c * FF_CHUNK, (c + 1) * FF_CHUNK)
        g = _dot(h, wg_ref[0, :, cols])
        u = _dot(h, wu_ref[0, :, cols])
        a = (g * _sigmoid(g) * u).astype(BF16)
        acc = acc + _dot(a, wd_ref[0, cols, :])
    o_ref[...] = x + (0.5 * g_ref[0]) * acc


def _mod_spec(seq, which):
    return pl.BlockSpec((1, 1, D_MODEL), lambda i: ((i * TILE // seq) * N_MOD + which, 0, 0))


def _resident(shape):
    return pl.BlockSpec(shape, lambda i: (0,) * len(shape), pipeline_mode=pl.Buffered(1))


def _row_spec(width):
    return pl.BlockSpec((TILE, width), lambda i: (i, 0))


def _ffn(x, mod, which, layer, wg, wu, wd, seq):
    t = x.shape[0]
    tm = FFN_TILE if seq % FFN_TILE == 0 else TILE
    row = pl.BlockSpec((tm, D_MODEL), lambda i: (i, 0))
    mod_row = lambda k: pl.BlockSpec((1, 1, D_MODEL), lambda i: ((i * tm // seq) * N_MOD + which + k, 0, 0))
    of_layer = lambda w: pl.BlockSpec((1,) + w.shape[1:], lambda i: (layer, 0, 0), pipeline_mode=pl.Buffered(1))
    return pl.pallas_call(
        _ffn_kernel,
        grid=(t // tm,),
        in_specs=[row, mod_row(0), mod_row(1), mod_row(2), of_layer(wg), of_layer(wu), of_layer(wd)],
        out_specs=row,
        out_shape=jax.ShapeDtypeStruct(x.shape, F32),
        compiler_params=_params(("parallel",)),
        name="ffn",
    )(x, mod, mod, mod, wg, wu, wd)


def _transposed_shape(batch, heads, seq):
    return jax.ShapeDtypeStruct((batch, heads, seq // TILE, LANES, TILE), BF16)


def _transposed_out_spec(heads, seq):
    chunks = seq // TILE
    return pl.BlockSpec((1, heads, 1, LANES, TILE), lambda i: (i // chunks, 0, i % chunks, 0, 0))


def _ab_proj_kernel(x_ref, sh_ref, sc_ref, w_ref, wqb_ref, wkn_ref, wvb_ref, gains_ref, lat_q_ref, lat_kv_ref,
                    g64_ref, g128_ref, cos_a_ref, sin_a_ref, cos_b_ref, sin_b_ref,
                    qat_ref, ka_ref, vat_ref, qbt_ref, kb_ref, vbt_ref):
    x = x_ref[...]
    h = _modulated(x, sh_ref[0], sc_ref[0]).astype(BF16)
    two = lambda r: jnp.concatenate([r, r], axis=1)
    cos_a, sin_a = two(cos_a_ref[...]), two(sin_a_ref[...])
    cos_b, sin_b = two(cos_b_ref[...]), two(sin_b_ref[...])
    gq_a, gk_a = gains_ref[0:1, :], gains_ref[1:2, :]
    gq_b, gk_b = gains_ref[2:3, :], gains_ref[3:4, :]
    g64, g128 = g64_ref[...], g128_ref[...]
    scale_a = DIFF_HEAD_DIM ** -0.5 * LOG2E
    scale_b = MLA_QK_DIM ** -0.5 * LOG2E
    heads_per_slab = SLAB // LANES
    o_cq = 3 * DIFF_W
    o_ckv = o_cq + MLA_Q_RANK
    o_kr = o_ckv + MLA_KV_RANK
    latent = {}

    def head_norm(t, gmat, dim, gain):
        ss = _group_sums(t * t, gmat)
        return t * lax.rsqrt(ss * (1.0 / dim) + EPS) * gain

    def project(cols):
        return lambda: _dot(h, w_ref[:, cols])

    def latent_inputs(y):
        c_q = y[:, :MLA_Q_RANK]
        c_q = c_q * lax.rsqrt(jnp.mean(c_q * c_q, axis=-1, keepdims=True) + EPS) * lat_q_ref[...]
        c_kv = y[:, MLA_Q_RANK:MLA_Q_RANK + MLA_KV_RANK]
        c_kv = c_kv * lax.rsqrt(jnp.mean(c_kv * c_kv, axis=-1, keepdims=True) + EPS) * lat_kv_ref[...]
        latent["q"], latent["kv"] = c_q.astype(BF16), c_kv.astype(BF16)
        latent["k_rope"] = two(y[:, MLA_Q_RANK + MLA_KV_RANK:])

    def diff_q(s):
        def consume(y):
            q = _rope(head_norm(y, g64, DIFF_HEAD_DIM, gq_a), cos_a, sin_a, 32)
            _put_transposed(qat_ref, heads_per_slab * s, q * scale_a)
        return consume

    def diff_k(s):
        def consume(y):
            k = _rope(head_norm(y, g64, DIFF_HEAD_DIM, gk_a), cos_a, sin_a, 32)
            ka_ref[:, s * SLAB:(s + 1) * SLAB] = k.astype(BF16)
        return consume

    def mla_q(s):
        def consume(y):
            q = _rope(head_norm(y, g128, MLA_QK_DIM, gq_b), cos_b, sin_b, 16)
            _put_transposed(qbt_ref, heads_per_slab * s, q * scale_b)
        return consume

    def mla_k(s):
        def consume(y):
            k = _rope(head_norm(y + latent["k_rope"], g128, MLA_QK_DIM, gk_b), cos_b, sin_b, 16)
            kb_ref[:, s * SLAB:(s + 1) * SLAB] = k.astype(BF16)
        return consume

    slab = lambda base, s: slice(base + s * SLAB, base + (s + 1) * SLAB)
    n_slabs = DIFF_W // SLAB
    stages = [(project(slice(o_cq, o_kr + LANES)), latent_inputs)]
    for s in range(n_slabs):
        stages.append((project(slab(0, s)), diff_q(s)))
        stages.append((project(slab(DIFF_W, s)), diff_k(s)))
        stages.append((project(slab(2 * DIFF_W, s)),
                       lambda y, s=s: _put_transposed(vat_ref, heads_per_slab * s, y)))
    for s in range(n_slabs):
        cols = slab(0, s)
        stages.append((lambda cols=cols: _dot(latent["q"], wqb_ref[:, cols]), mla_q(s)))
        stages.append((lambda cols=cols: _dot(latent["kv"], wkn_ref[:, cols]), mla_k(s)))
        stages.append((lambda cols=cols: _dot(latent["kv"], wvb_ref[:, cols]),
                       lambda y, s=s: _put_transposed(vbt_ref, heads_per_slab * s, y)))
    _run_staged(stages)


def _ab_proj(x, mod, w_all, wqb, wkn, wvb, gains, lat_q, lat_kv, g64, g128, tables, batch, seq):
    t = x.shape[0]
    row = _row_spec(D_MODEL)
    tab = _row_spec(LANES)
    rows_out = jax.ShapeDtypeStruct((t, D_MODEL), BF16)
    tr_out = _transposed_shape(batch, DIFF_HEADS, seq)
    tr_spec = _transposed_out_spec(DIFF_HEADS, seq)
    return pl.pallas_call(
        _ab_proj_kernel,
        grid=(t // TILE,),
        in_specs=[row, _mod_spec(seq, 3), _mod_spec(seq, 4),
                  _resident(w_all.shape), _resident(wqb.shape), _resident(wkn.shape), _resident(wvb.shape),
                  _resident(gains.shape), _resident(lat_q.shape), _resident(lat_kv.shape),
                  _resident(g64.shape), _resident(g128.shape), tab, tab, tab, tab],
        out_specs=[tr_spec, row, tr_spec, tr_spec, row, tr_spec],
        out_shape=[tr_out, rows_out, tr_out, tr_out, rows_out, tr_out],
        compiler_params=_params(("parallel",)),
        name="ab_proj",
    )(x, mod, mod, w_all, wqb, wkn, wvb, gains, lat_q, lat_kv, g64, g128, *tables)


def _fox_proj_kernel(seq, x_ref, sh_ref, sc_ref, w_ref, gains_ref, bf_ref, g64_ref, tri_ref,
                     qft_ref, kf_ref, vft_ref, og_ref, carry_ref):
    i = pl.program_id(0)
    tm = x_ref.shape[0]

    @pl.when((i * tm) % seq == 0)
    def _():
        carry_ref[...] = jnp.zeros(carry_ref.shape, F32)

    x = x_ref[...]
    h = _modulated(x, sh_ref[0], sc_ref[0]).astype(BF16)
    lane = lax.broadcasted_iota(jnp.int32, (tm, LANES), 1)
    gq, gk = gains_ref[0:1, :], gains_ref[1:2, :]
    g64 = g64_ref[...]
    scale = FOX_HEAD_DIM ** -0.5 * LOG2E
    heads_per_slab = SLAB // FOX_HEAD_DIM
    gate = {}

    def project(cols):
        return lambda: _dot(h, w_ref[:, cols])

    def forget_cumsum(f):
        z = f + bf_ref[...]
        log_f = jnp.where(lane < FOX_HEADS, jnp.minimum(z, 0.0) - jnp.log1p(jnp.exp(-jnp.abs(z))), 0.0)
        tri = tri_ref[...]
        hi, mid, lo = _split3(log_f)
        cum = _dot(tri, hi) + _dot(tri, mid) + _dot(tri, lo) + carry_ref[0:1, :]
        carry_ref[0:1, :] = cum[tm - 1:tm, :]
        gate["cum"] = cum * LOG2E

    def head_norm(t, gain):
        ss = _group_sums(t * t, g64)
        return t * lax.rsqrt(ss * (1.0 / FOX_HEAD_DIM) + EPS) * gain

    def bias_lanes(pieces, ones_at, pieces_at, sign):
        blk = jnp.zeros((tm, LANES), F32)
        for ln in ones_at:
            blk = jnp.where(lane == ln, 1.0, blk)
        for ln, p in zip(pieces_at, pieces):
            blk = jnp.where(lane == ln, sign * p.astype(F32), blk)
        return blk

    def head_blocks(t, s, is_q):
        for j in range(heads_per_slab):
            hd = s * heads_per_slab + j
            th = t[:, (j // 2) * LANES:(j // 2 + 1) * LANES]
            if j % 2:
                th = pltpu.roll(th, FOX_HEAD_DIM, 1)
            pieces = _split3(jnp.broadcast_to(gate["cum"][:, hd:hd + 1], (tm, LANES)))
            if is_q:
                aug = bias_lanes(pieces, FOX_ONES_Q, FOX_BIAS_Q, 1.0)
                _put_transposed(qft_ref, hd, jnp.where(lane < FOX_HEAD_DIM, th, aug))
            else:
                aug = bias_lanes(pieces, FOX_BIAS_Q, FOX_ONES_Q, -1.0)
                kf_ref[:, hd * LANES:(hd + 1) * LANES] = jnp.where(lane < FOX_HEAD_DIM, th, aug).astype(BF16)

    def store_gate(s):
        def consume(y):
            og_ref[:, s * SLAB:(s + 1) * SLAB] = y
        return consume

    slab = lambda base, s: slice(base + s * SLAB, base + (s + 1) * SLAB)
    stages = [(project(slice(4 * FOX_W, 4 * FOX_W + LANES)), forget_cumsum)]
    for s in range(FOX_W // SLAB):
        stages.append((project(slab(2 * FOX_W, s)),
                       lambda y, s=s: _put_transposed(vft_ref, (SLAB // LANES) * s, y)))
        stages.append((project(slab(3 * FOX_W, s)), store_gate(s)))
    for s in range(FOX_W // SLAB):
        stages.append((project(slab(0, s)), lambda y, s=s: head_blocks(head_norm(y, gq) * scale, s, True)))
        stages.append((project(slab(FOX_W, s)), lambda y, s=s: head_blocks(head_norm(y, gk), s, False)))
    _run_staged(stages)


def _fox_proj(x, mod, w_fox, gains, bf_row, g64, tri, batch, seq):
    t = x.shape[0]
    row = _row_spec(D_MODEL)
    return pl.pallas_call(
        functools.partial(_fox_proj_kernel, seq),
        grid=(t // TILE,),
        in_specs=[row, _mod_spec(seq, 3), _mod_spec(seq, 4), _resident(w_fox.shape),
                  _resident(gains.shape), _resident(bf_row.shape), _resident(g64.shape), _resident(tri.shape)],
        out_specs=[_transposed_out_spec(FOX_HEADS, seq), _row_spec(FOX_HEADS * LANES),
                   _transposed_out_spec(FOX_HEADS // 2, seq), row],
        out_shape=[_transposed_shape(batch, FOX_HEADS, seq), jax.ShapeDtypeStruct((t, FOX_HEADS * LANES), BF16),
                   _transposed_shape(batch, FOX_HEADS // 2, seq), jax.ShapeDtypeStruct((t, FOX_W), F32)],
        scratch_shapes=[pltpu.VMEM((8, LANES), F32)],
        compiler_params=_params(("arbitrary",)),
        name="fox_proj",
    )(x, mod, mod, w_fox, gains, bf_row, g64, tri)


def _tile_order(nq):
    below = [(qi, j) for qi in range(1, nq) for j in range(qi)]
    peeled = len(below) % TILES_PER_TRIP
    unrolled = [(i, i, True) for i in range(nq)] + [(qi, j, False) for qi, j in below[:peeled]]
    looped = below[peeled:]
    table = np.asarray(looped + looped[-1:], np.int32).reshape(-1) if looped else np.zeros((2,), np.int32)
    return unrolled, len(looped), table


def _flash_head(tbl_ref, nq, q_at, k_refs, vt_at, s_buf, m_ref, l_ref, acc_ref):
    n = len(k_refs)
    unrolled, n_looped, _ = _tile_order(nq)

    half = TILE // 2
    lower = lambda t: t[:, :half]
    upper = lambda t: t[:, half:]
    join = lambda a, b: jnp.concatenate([a, b], axis=1)

    def issue_diagonal(qi, slot):
        off = qi * TILE
        key = lax.broadcasted_iota(jnp.int32, (half, half), 0)
        query = lax.broadcasted_iota(jnp.int32, (half, half), 1)
        causal = key <= query
        maxima = []
        for e in range(n):
            qt = q_at[e](qi)
            top = _dot(k_refs[e][off:off + half, :], qt)
            top = join(jnp.where(causal, lower(top), NEG_INF), upper(top))
            bot = jnp.where(causal, _dot(k_refs[e][off + half:off + TILE, :], upper(qt)), NEG_INF)
            s_buf[slot, e, 0:half, :] = top
            s_buf[slot, e, half:TILE, half:TILE] = bot
            top_max = jnp.max(top, axis=0, keepdims=True)
            maxima.append(join(lower(top_max), jnp.maximum(upper(top_max), jnp.max(bot, axis=0, keepdims=True))))
        return tuple(maxima)

    def issue(qi, j, slot, diagonal):
        if diagonal:
            return issue_diagonal(qi, slot)
        off = pl.multiple_of(j * TILE, TILE)
        maxima = []
        for e in range(n):
            s = _dot(k_refs[e][pl.ds(off, TILE), :], q_at[e](qi))
            s_buf[slot, e] = s
            maxima.append(jnp.max(s, axis=0, keepdims=True))
        return tuple(maxima)

    ones_rows = 16
    value_rows = acc_ref.shape[2]

    def weigh(vt, p):
        lhs = jnp.concatenate([vt, jnp.ones((ones_rows, vt.shape[1]), BF16)], axis=0)
        r = _dot(lhs, p.astype(BF16))
        return r[:value_rows], r[value_rows:value_rows + 1]

    def absorb(qi, j, slot, maxima, first):
        for e in range(n):
            if first:
                m_new = maxima[e]
                p_top = jnp.exp2(s_buf[slot, e, 0:half, :] - m_new)
                p_bot = jnp.exp2(s_buf[slot, e, half:TILE, half:TILE] - upper(m_new))
                vt = vt_at[e](j)
                acc_top, l_top = weigh(lower(vt), p_top)
                acc_bot, l_bot = weigh(upper(vt), p_bot)
                l = join(lower(l_top), upper(l_top) + l_bot)
                acc = join(lower(acc_top), upper(acc_top) + acc_bot)
            else:
                m = m_ref[e, qi]
                m_new = jnp.maximum(m, maxima[e])
                alpha = jnp.exp2(m - m_new)
                pv, p_sum = weigh(vt_at[e](j), jnp.exp2(s_buf[slot, e] - m_new))
                l = alpha * l_ref[e, qi] + p_sum
                acc = alpha * acc_ref[e, qi] + pv
            m_ref[e, qi] = m_new
            l_ref[e, qi] = l
            acc_ref[e, qi] = acc

    def looped(t):
        return tbl_ref[2 * t], tbl_ref[2 * t + 1]

    pending = issue(*unrolled[0][:2], 0, unrolled[0][2])
    for idx, (qi, j, diagonal) in enumerate(unrolled):
        slot = idx % 2
        ahead = None
        if idx + 1 < len(unrolled):
            nxt = unrolled[idx + 1]
            ahead = issue(nxt[0], nxt[1], 1 - slot, nxt[2])
        elif n_looped:
            ahead = issue(*looped(0), 1 - slot, False)
        absorb(qi, j, slot, pending, diagonal)
        pending = ahead

    if n_looped:
        base = len(unrolled) % 2

        def body(trip, held):
            for k in range(TILES_PER_TRIP):
                t = TILES_PER_TRIP * trip + k
                slot = (base + k) % 2
                ahead = issue(*looped(t + 1), 1 - slot, False)
                absorb(*looped(t), slot, held, False)
                held = ahead
            return held

        lax.fori_loop(0, n_looped // TILES_PER_TRIP, body, pending)

    return lambda e, qi: acc_ref[e, qi] / l_ref[e, qi]


def _diff_attn_kernel(lambda_init, nq, tbl_ref, qt_ref, k_ref, vt_ref, lam_ref, subln_ref, o_ref, *scratch):
    feat = lax.broadcasted_iota(jnp.int32, (LANES, TILE), 0)
    zero = jnp.zeros((LANES, TILE), BF16)
    q1 = lambda qi: jnp.where(feat < DIFF_HEAD_DIM, qt_ref[0, 0, qi], zero)
    q2 = lambda qi: jnp.where(feat < DIFF_HEAD_DIM, zero, qt_ref[0, 0, qi])
    vt = lambda j: vt_ref[0, 0, j]
    out_t = _flash_head(tbl_ref, nq, [q1, q2], [k_ref, k_ref], [vt, vt], *scratch)
    lv = lam_ref[...]
    lam = (jnp.exp(jnp.sum(lv[0:1] * lv[1:2], axis=-1, keepdims=True))
           - jnp.exp(jnp.sum(lv[2:3] * lv[3:4], axis=-1, keepdims=True)) + lambda_init)
    for qi in range(nq):
        o = (out_t(0, qi) - lam * out_t(1, qi)).T
        o = o * lax.rsqrt(jnp.mean(o * o, axis=-1, keepdims=True) + EPS) * subln_ref[...]
        o_ref[qi * TILE:(qi + 1) * TILE, :] = (o * (1.0 - lambda_init)).astype(BF16)


def _mla_attn_kernel(nq, tbl_ref, qt_ref, k0_ref, k1_ref, vt_ref, o_ref, *scratch):
    out_t = _flash_head(tbl_ref, nq, [lambda qi: qt_ref[0, 0, qi], lambda qi: qt_ref[0, 1, qi]], [k0_ref, k1_ref],
                        [lambda j: vt_ref[0, 0, j], lambda j: vt_ref[0, 1, j]], *scratch)
    for qi in range(nq):
        for e in range(2):
            o_ref[qi * TILE:(qi + 1) * TILE, e * LANES:(e + 1) * LANES] = out_t(e, qi).T.astype(BF16)


def _fox_attn_kernel(nq, tbl_ref, qt_ref, k0_ref, k1_ref, vt_ref, og_ref, o_ref, *scratch):
    vt = [lambda j, e=e: vt_ref[0, 0, j, e * FOX_HEAD_DIM:(e + 1) * FOX_HEAD_DIM, :] for e in range(2)]
    out_t = _flash_head(tbl_ref, nq, [lambda qi: qt_ref[0, 0, qi], lambda qi: qt_ref[0, 1, qi]], [k0_ref, k1_ref],
                        vt, *scratch)
    for qi in range(nq):
        rows = slice(qi * TILE, (qi + 1) * TILE)
        o = jnp.concatenate([out_t(0, qi), out_t(1, qi)], axis=0).T
        o_ref[rows, :] = (o * _sigmoid(og_ref[rows, :])).astype(BF16)


def _head_tiles_spec(heads, seq):
    return pl.BlockSpec((1, heads, seq // TILE, LANES, TILE), lambda b, h, tbl: (b, h, 0, 0, 0))


def _head_rows_spec(seq, width, head_of=lambda h: h):
    return pl.BlockSpec((seq, width), lambda b, h, tbl: (b, head_of(h)))


def _attn_call(body, name, batch, groups, seq, in_specs, out_spec, out_shape, operands, value_rows=LANES):
    nq = seq // TILE
    table = jnp.asarray(_tile_order(nq)[2])
    stat = pltpu.VMEM((2, nq, 1, TILE), F32)
    return pl.pallas_call(
        functools.partial(body, nq),
        grid_spec=pltpu.PrefetchScalarGridSpec(
            num_scalar_prefetch=1,
            grid=(batch, groups),
            in_specs=in_specs,
            out_specs=out_spec,
            scratch_shapes=[pltpu.VMEM((2, 2, TILE, TILE), F32),
                            stat, stat, pltpu.VMEM((2, nq, value_rows, TILE), F32)],
        ),
        out_shape=out_shape,
        compiler_params=_params(("parallel", "parallel")),
        name=name,
    )(table, *operands)


def _diff_attn(qat, ka, vat, lam_rows, subln, lambda_init, batch, seq):
    const = lambda shape: pl.BlockSpec(shape, lambda b, h, tbl: (0, 0))
    return _attn_call(
        functools.partial(_diff_attn_kernel, lambda_init), "diff_attn", batch, DIFF_HEADS, seq,
        [_head_tiles_spec(1, seq), _head_rows_spec(seq, LANES), _head_tiles_spec(1, seq),
         const(lam_rows.shape), const(subln.shape)],
        _head_rows_spec(seq, LANES), jax.ShapeDtypeStruct(ka.shape, BF16), (qat, ka, vat, lam_rows, subln))


def _mla_attn(qbt, kb, vbt, batch, seq):
    return _attn_call(
        _mla_attn_kernel, "mla_attn", batch, MLA_HEADS // 2, seq,
        [_head_tiles_spec(2, seq), _head_rows_spec(seq, LANES, lambda p: 2 * p),
         _head_rows_spec(seq, LANES, lambda p: 2 * p + 1), _head_tiles_spec(2, seq)],
        _head_rows_spec(seq, 2 * LANES), jax.ShapeDtypeStruct(kb.shape, BF16), (qbt, kb, kb, vbt))


def _fox_attn(qft, kf, vft, og, batch, seq):
    return _attn_call(
        _fox_attn_kernel, "fox_attn", batch, FOX_HEADS // 2, seq,
        [_head_tiles_spec(2, seq), _head_rows_spec(seq, LANES, lambda p: 2 * p),
         _head_rows_spec(seq, LANES, lambda p: 2 * p + 1), _head_tiles_spec(1, seq), _head_rows_spec(seq, LANES)],
        _head_rows_spec(seq, LANES), jax.ShapeDtypeStruct(og.shape, BF16), (qft, kf, kf, vft, og),
        value_rows=FOX_HEAD_DIM)


def _out_proj_kernel(n_in, *refs):
    x_ref, g_ref = refs[0], refs[1]
    o_refs = refs[2:2 + n_in]
    w_refs = refs[2 + n_in:2 + 2 * n_in]
    out_ref = refs[2 + 2 * n_in]
    mix = _dot(o_refs[0][...], w_refs[0][...])
    for o_ref, w_ref in zip(o_refs[1:], w_refs[1:]):
        mix = mix + _dot(o_ref[...], w_ref[...])
    out_ref[...] = x_ref[...] + g_ref[0] * mix


def _out_proj(x, mod, os, ws, seq):
    t = x.shape[0]
    row = _row_spec(D_MODEL)
    return pl.pallas_call(
        functools.partial(_out_proj_kernel, len(os)),
        grid=(t // TILE,),
        in_specs=[row, _mod_spec(seq, 5)] + [_row_spec(o.shape[1]) for o in os] + [_resident(w.shape) for w in ws],
        out_specs=row,
        out_shape=jax.ShapeDtypeStruct(x.shape, F32),
        compiler_params=_params(("parallel",)),
        name="out_proj",
    )(x, mod, *os, *ws)


def _lane_rows(rows):
    width = max(r.shape[0] for r in rows)
    out = jnp.zeros((8, width), F32)
    for i, r in enumerate(rows):
        out = out.at[i, :r.shape[0]].set(r.astype(F32))
    return out


def _block_diag_ones(n, group):
    idx = np.arange(n) // group
    return jnp.asarray((idx[:, None] == idx[None, :]).astype(np.float32), BF16)


def _pad_heads(w, heads, dim):
    r = w.shape[0]
    w = w.reshape(r, heads, dim)
    return jnp.pad(w, ((0, 0), (0, 0), (0, LANES - dim))).reshape(r, heads * LANES)


def kernel(x, c, positions, ada_w, ada_b, ff1_gate, ff1_up, ff1_down, ff2_gate, ff2_up, ff2_down, ab_w_in, mla_w_qb, mla_w_kvb, mla_q_lat_g, mla_kv_lat_g, diff_q_g, diff_k_g, mla_q_g, mla_k_g, diff_lam_q1, diff_lam_k1, diff_lam_q2, diff_lam_k2, diff_subln_g, ab_w_out, fox_w_in, fox_b_f, fox_q_g, fox_k_g, fox_w_out):
    batch, seq, d = x.shape
    depth = ada_w.shape[0]
    t = batch * seq
    assert d == D_MODEL and seq % TILE == 0 and batch <= 8

    c_pad = jnp.zeros((8, d), F32).at[:batch].set(c)
    mod_all = _adaln(c_pad, ada_w, ada_b)

    inv_a = ROPE_THETA ** (-jnp.arange(0, DIFF_HEAD_DIM, 2, dtype=F32) / DIFF_HEAD_DIM)
    inv_b = ROPE_THETA ** (-jnp.arange(0, MLA_ROPE_DIM, 2, dtype=F32) / MLA_ROPE_DIM)
    inv_rows = jnp.zeros((8, LANES), F32).at[0, :32].set(inv_a).at[0, 64:80].set(inv_b)
    lane = np.arange(LANES)
    sgn_a = np.where(lane % 64 < 32, -1.0, 1.0)
    sgn_b = np.where((lane >= 64) & (lane < 80), -1.0, np.where((lane >= 80) & (lane < 96), 1.0, 0.0))
    sgn_rows = jnp.asarray(np.stack([sgn_a, sgn_b] + [np.zeros(LANES)] * 6), F32)
    pos_b = jnp.broadcast_to(positions.astype(F32).reshape(t, 1), (t, LANES))
    tables = _rope_tables(pos_b, inv_rows, sgn_rows)

    g64 = _block_diag_ones(SLAB, 64)
    g128 = _block_diag_ones(SLAB, LANES)
    bf = lambda w: w.astype(BF16)
    ff1 = [_to_bf16(w) for w in (ff1_gate, ff1_up, ff1_down)]
    ff2 = [_to_bf16(w) for w in (ff2_gate, ff2_up, ff2_down)]

    xt = x.reshape(t, d)
    for l in range(depth):
        mod = mod_all[l, :batch].reshape(batch * N_MOD, 1, d)
        xt = _ffn(xt, mod, 0, l, *ff1, seq)
        if l % 2 == 0:
            e = l // 2
            lambda_init = 0.8 - 0.6 * math.exp(-0.3 * l)
            w_in = ab_w_in[e]
            o_cq = 3 * DIFF_W
            o_kr = o_cq + MLA_Q_RANK + MLA_KV_RANK
            kr_pad = jnp.zeros((d, LANES), F32).at[:, MLA_NOPE_DIM:MLA_QK_DIM].set(w_in[:, o_kr:])
            w_all = bf(jnp.concatenate([w_in[:, :o_kr], kr_pad], axis=1))
            wqb = bf(_pad_heads(mla_w_qb[e], MLA_HEADS, MLA_QK_DIM))
            wkv = mla_w_kvb[e].reshape(MLA_KV_RANK, MLA_HEADS, MLA_NOPE_DIM + MLA_V_DIM)
            wkn = bf(_pad_heads(wkv[:, :, :MLA_NOPE_DIM].reshape(MLA_KV_RANK, -1), MLA_HEADS, MLA_NOPE_DIM))
            wvb = bf(wkv[:, :, MLA_NOPE_DIM:].reshape(MLA_KV_RANK, -1))
            pad128 = lambda g: jnp.pad(g, (0, LANES - g.shape[0]))
            gains = _lane_rows([jnp.tile(diff_q_g[e], 4), jnp.tile(diff_k_g[e], 4),
                                jnp.tile(pad128(mla_q_g[e]), 2), jnp.tile(pad128(mla_k_g[e]), 2)])
            qat, ka, vat, qbt, kb, vbt = _ab_proj(
                xt, mod, w_all, wqb, wkn, wvb, gains, mla_q_lat_g[e].reshape(1, -1), mla_kv_lat_g[e].reshape(1, -1),
                g64, g128, tables, batch, seq)
            lam_rows = _lane_rows([pad128(diff_lam_q1[e]), pad128(diff_lam_k1[e]),
                                   pad128(diff_lam_q2[e]), pad128(diff_lam_k2[e])])
            o_a = _diff_attn(qat, ka, vat, lam_rows, diff_subln_g[e].reshape(1, -1), lambda_init, batch, seq)
            o_b = _mla_attn(qbt, kb, vbt, batch, seq)
            w_out = bf(ab_w_out[e])
            xt = _out_proj(xt, mod, [o_a, o_b], [w_out[:DIFF_W], w_out[DIFF_W:]], seq)
        else:
            o = l // 2
            w_fox = bf(jnp.pad(fox_w_in[o], ((0, 0), (0, LANES - FOX_HEADS))))
            gains = _lane_rows([jnp.tile(fox_q_g[o], 4), jnp.tile(fox_k_g[o], 4)])
            bf_row = jnp.pad(fox_b_f[o], (0, LANES - FOX_HEADS)).reshape(1, LANES)
            tri = jnp.asarray(np.tril(np.ones((TILE, TILE), np.float32)), BF16)
            qft, kf, vft, og = _fox_proj(xt, mod, w_fox, gains, bf_row, g64, tri, batch, seq)
            o_f = _fox_attn(qft, kf, vft, og, batch, seq)
            xt = _out_proj(xt, mod, [o_f], [bf(fox_w_out[o])], seq)
        xt = _ffn(xt, mod, 6, l, *ff2, seq)
    return xt.reshape(batch, seq, d)
```
